```python
import math
import jax, jax.numpy as jnp
from jax import lax
import numpy as np

D_MODEL = 1024
BATCH = 4
SEQ = 8192
DEPTH = 4

HEAD_DIM = 64
HG_W = D_MODEL // 4
HG_HEADS = HG_W // HEAD_DIM
NSA_W = D_MODEL // 2
NSA_HEADS = NSA_W // HEAD_DIM
NSA_GQA = 4
NSA_KV_HEADS = NSA_HEADS // NSA_GQA
KV_W = NSA_KV_HEADS * HEAD_DIM
GM_W = D_MODEL - HG_W - NSA_W
GM_HEADS = GM_W // HEAD_DIM
D_MIX = HG_W + NSA_W + GM_W
IN_COLS = 4 * HG_W + NSA_W + 6 * KV_W + 3 * NSA_HEADS + 2 * GM_W
D_FF = 4 * D_MODEL
ROPE_THETA = 10000.0
HGRN_CHUNK = 64
CMP_LEN = 32
CMP_STRIDE = 16
CMP_HIDDEN = 256
SLC_BLOCK = 64
SLC_TOPN = 16
WINDOW = 512
Q_BLOCK = 128
GMLP_CHUNK = 128
DN_ALPHA = (2 * DEPTH) ** 0.25
DN_BETA = (8 * DEPTH) ** -0.25
NEG = -1e30
BIG = 1e30
F_MIN = 1e-30

kernel_name = 'hymba_style_hgrn2_nsa_gmlp_deepnorm_adaln'


def layer_norm(x, w, b, eps=1e-5):
    xf = x.astype(jnp.float32)
    mu = jnp.mean(xf, axis=-1, keepdims=True)
    var = jnp.mean(jnp.square(xf - mu), axis=-1, keepdims=True)
    y = (xf - mu) * lax.rsqrt(var + eps)
    return (y * w.astype(jnp.float32) + b.astype(jnp.float32)).astype(x.dtype)


def rope(a, cos, sin):
    half = a.shape[-1] // 2
    a1, a2 = a[..., :half], a[..., half:]
    return jnp.concatenate([a1 * cos - a2 * sin, a2 * cos + a1 * sin], axis=-1).astype(a.dtype)


def hgrn2(q, f_raw, i, g, lb, norm_w):
    B_, S_, H, dk = q.shape
    dt = q.dtype
    C = HGRN_CHUNK
    lb = lb.astype(jnp.float32).reshape(H, dk)
    z = f_raw.astype(jnp.float32)
    f = lb + (1.0 - lb) * jax.nn.sigmoid(z)
    log_f = jnp.log(jnp.maximum(f, F_MIN))
    k = (1.0 - lb) * jax.nn.sigmoid(-z)
    qf = jax.nn.silu(q.astype(jnp.float32))
    v = i.astype(jnp.float32)
    nC = S_ // C

    def chunks(a):
        return a.reshape(B_, nC, C, H, dk).transpose(1, 0, 3, 2, 4)

    qc, kc, vc = chunks(qf), chunks(k), chunks(v)
    bc = jnp.cumsum(chunks(log_f), axis=3)
    causal = jnp.tril(jnp.ones((C, C), bool))

    def step(state, inp):
        q_, k_, v_, b_ = inp
        rel = b_[:, :, :, None, :] - b_[:, :, None, :, :]
        decay = jnp.exp(jnp.where(causal[:, :, None], rel, NEG))
        attn = jnp.einsum('bhtk,bhtsk,bhsk->bhts', q_, decay, k_)
        o = attn @ v_ + jnp.einsum('bhtk,bhkv->bhtv', q_ * jnp.exp(b_), state)
        b_last = b_[:, :, -1:, :]
        state = (jnp.exp(b_last[:, :, 0, :])[..., None] * state
                 + jnp.einsum('bhsk,bhsv->bhkv', k_ * jnp.exp(b_last - b_), v_))
        return state, o

    s0 = jnp.zeros((B_, H, dk, dk), jnp.float32)
    _, o = lax.scan(step, s0, (qc, kc, vc, bc))
    o = o.transpose(1, 0, 3, 2, 4).reshape(B_, S_, H, dk)
    o = o * lax.rsqrt(jnp.mean(jnp.square(o), axis=-1, keepdims=True) + 1e-6) * norm_w.astype(jnp.float32)
    o = o * jax.nn.silu(g.astype(jnp.float32))
    return o.reshape(B_, S_, H * dk).astype(dt)


def compress(a, pe, w1, w2):
    B_, S_, Hk, dh = a.shape
    seg = a.reshape(B_, S_ // CMP_STRIDE, CMP_STRIDE, Hk, dh)
    blk = jnp.concatenate([seg[:, :-1], seg[:, 1:]], axis=2)
    blk = blk + pe[None, None, :, None, :]
    nC = blk.shape[1]
    blk = blk.transpose(0, 1, 3, 2, 4).reshape(B_, nC, Hk, CMP_LEN * dh)
    return jax.nn.gelu(blk @ w1) @ w2


def nsa(q, k_cmp, v_cmp, k_slc, v_slc, k_win, v_win, gates, pe_k, w1k, w2k, pe_v, w1v, w2v):
    B_, S_, H, dh = q.shape
    Hk, G = NSA_KV_HEADS, NSA_GQA
    dt = q.dtype
    qg = (q * (dh ** -0.5)).reshape(B_, S_, Hk, G, dh)
    kc = compress(k_cmp, pe_k, w1k, w2k)
    vc = compress(v_cmp, pe_v, w1v, w2v)
    nC = kc.shape[1]
    nS = S_ // SLC_BLOCK
    topn = min(SLC_TOPN, nS)
    cmp_end = jnp.arange(nC) * CMP_STRIDE + CMP_LEN - 1
    cs = np.arange(nC) * CMP_STRIDE
    ss = np.arange(nS) * SLC_BLOCK
    overlap = jnp.asarray(((cs[:, None] < ss[None, :] + SLC_BLOCK) & (ss[None, :] < cs[:, None] + CMP_LEN)).astype(np.float32))
    ks_b = k_slc.reshape(B_, nS, SLC_BLOCK, Hk, dh).transpose(0, 3, 1, 2, 4)
    vs_b = v_slc.reshape(B_, nS, SLC_BLOCK, Hk, dh).transpose(0, 3, 1, 2, 4)
    kw_pad = jnp.pad(k_win, ((0, 0), (WINDOW, 0), (0, 0), (0, 0)))
    vw_pad = jnp.pad(v_win, ((0, 0), (WINDOW, 0), (0, 0), (0, 0)))
    bidx = jnp.arange(B_)[:, None, None, None]
    hidx = jnp.arange(Hk)[None, :, None, None]
    j = jnp.arange(nS)
    in_blk = jnp.arange(SLC_BLOCK)
    win_off = jnp.arange(WINDOW + Q_BLOCK) - WINDOW

    def block(qb):
        q0 = qb * Q_BLOCK
        t = q0 + jnp.arange(Q_BLOCK)
        q_blk = lax.dynamic_slice_in_dim(qg, q0, Q_BLOCK, axis=1)
        g_blk = lax.dynamic_slice_in_dim(gates, q0, Q_BLOCK, axis=1)
        s = jnp.einsum('bqhgd,bnhd->bhgqn', q_blk, kc).astype(jnp.float32)
        m = cmp_end[None, :] <= t[:, None]
        p_cmp = jax.nn.softmax(jnp.where(m, s, NEG), axis=-1) * m
        o_cmp = jnp.einsum('bhgqn,bnhd->bqhgd', p_cmp.astype(dt), vc)
        imp = jnp.einsum('bhgqn,nj->bhqj', p_cmp, overlap)
        tb = t // SLC_BLOCK
        forced = (j[None] == 0) | (j[None] == tb[:, None]) | (j[None] == tb[:, None] - 1)
        score = jnp.where(j[None] > tb[:, None], NEG, jnp.where(forced, BIG, imp))
        _, idx = lax.top_k(score, topn)
        k_sel = ks_b[bidx, hidx, idx].reshape(B_, Hk, Q_BLOCK, topn * SLC_BLOCK, dh)
        v_sel = vs_b[bidx, hidx, idx].reshape(B_, Hk, Q_BLOCK, topn * SLC_BLOCK, dh)
        kpos = (idx[..., None] * SLC_BLOCK + in_blk).reshape(B_, Hk, Q_BLOCK, topn * SLC_BLOCK)
        s = jnp.einsum('bqhgd,bhqkd->bhgqk', q_blk, k_sel).astype(jnp.float32)
        m = (kpos <= t[None, None, :, None])[:, :, None]
        p = jax.nn.softmax(jnp.where(m, s, NEG), axis=-1)
        o_slc = jnp.einsum('bhgqk,bhqkd->bqhgd', p.astype(dt), v_sel)
        k_w = lax.dynamic_slice_in_dim(kw_pad, q0, WINDOW + Q_BLOCK, axis=1)
        v_w = lax.dynamic_slice_in_dim(vw_pad, q0, WINDOW + Q_BLOCK, axis=1)
        kp = q0 + win_off
        m = (kp[None] <= t[:, None]) & (kp[None] > t[:, None] - WINDOW) & (kp[None] >= 0)
        s = jnp.einsum('bqhgd,bkhd->bhgqk', q_blk, k_w).astype(jnp.float32)
        p = jax.nn.softmax(jnp.where(m, s, NEG), axis=-1)
        o_win = jnp.einsum('bhgqk,bkhd->bqhgd', p.astype(dt), v_w)
        out = g_blk[..., 0:1] * o_cmp + g_blk[..., 1:2] * o_slc + g_blk[..., 2:3] * o_win
        return out.astype(dt)

    out = lax.map(block, jnp.arange(S_ // Q_BLOCK))
    return out.transpose(1, 0, 2, 3, 4, 5).reshape(B_, S_, H * dh)


def gmlp(u_raw, v_raw, nw, nb, w_s, b_s):
    B_, S_, W = u_raw.shape
    u = jax.nn.gelu(u_raw)
    v = layer_norm(jax.nn.gelu(v_raw), nw, nb)
    v = v.reshape(B_, S_ // GMLP_CHUNK, GMLP_CHUNK, GM_HEADS, W // GM_HEADS)
    w = w_s * jnp.tril(jnp.ones((GMLP_CHUNK, GMLP_CHUNK), w_s.dtype))
    sv = jnp.einsum('gts,bnsgc->bntgc', w, v) + b_s.T[None, None, :, :, None]
    return u * sv.reshape(B_, S_, W)


def setup_inputs(seed: int = 0) -> dict:
    key = jax.random.key(seed)
    ks = jax.random.split(key, 32)
    L, D = DEPTH, D_MODEL

    def nrm(k, shape, scale):
        return jax.random.normal(k, shape, jnp.float32) * scale

    offs = jax.random.randint(ks[2], (BATCH, 1), 0, 4096, dtype=jnp.int32)
    return {
        'x': nrm(ks[0], (BATCH, SEQ, D), 1.0),
        'c': nrm(ks[1], (BATCH, D), 1.0),
        'positions': offs + jnp.arange(SEQ, dtype=jnp.int32)[None, :],
        'w_in': nrm(ks[3], (L, D, IN_COLS), D ** -0.5),
        'w_o': nrm(ks[4], (L, D_MIX, D), D_MIX ** -0.5 * DN_BETA),
        'hgrn_lower_bounds': 1.0 + nrm(ks[5], (L, HG_W), 0.1),
        'hgrn_norm_w': 1.0 + nrm(ks[6], (L, HEAD_DIM), 0.02),
        'cmp_pe_k': nrm(ks[7], (L, CMP_LEN, HEAD_DIM), 0.02),
        'cmp_w1_k': nrm(ks[8], (L, CMP_LEN * HEAD_DIM, CMP_HIDDEN), (CMP_LEN * HEAD_DIM) ** -0.5),
        'cmp_w2_k': nrm(ks[9], (L, CMP_HIDDEN, HEAD_DIM), CMP_HIDDEN ** -0.5),
        'cmp_pe_v': nrm(ks[10], (L, CMP_LEN, HEAD_DIM), 0.02),
        'cmp_w1_v': nrm(ks[11], (L, CMP_LEN * HEAD_DIM, CMP_HIDDEN), (CMP_LEN * HEAD_DIM) ** -0.5),
        'cmp_w2_v': nrm(ks[12], (L, CMP_HIDDEN, HEAD_DIM), CMP_HIDDEN ** -0.5),
        'gmlp_norm_w': 1.0 + nrm(ks[13], (L, GM_W), 0.02),
        'gmlp_norm_b': nrm(ks[14], (L, GM_W), 0.02),
        'gmlp_w_s': nrm(ks[15], (L, GM_HEADS, GMLP_CHUNK, GMLP_CHUNK), GMLP_CHUNK ** -0.5),
        'gmlp_b_s': 1.0 + nrm(ks[16], (L, GM_HEADS, GMLP_CHUNK), 0.02),
        'w_ff1': nrm(ks[17], (L, D, D_FF), D ** -0.5),
        'w_ff2': nrm(ks[18], (L, D_FF, D), D_FF ** -0.5 * DN_BETA),
        'w_ada': nrm(ks[19], (L, D, 6 * D), 0.1 * D ** -0.5),
        'b_ada': nrm(ks[20], (L, 6 * D), 0.02),
        'ln1_w': 1.0 + nrm(ks[21], (L, D), 0.02),
        'ln1_b': nrm(ks[22], (L, D), 0.02),
        'ln2_w': 1.0 + nrm(ks[23], (L, D), 0.02),
        'ln2_b': nrm(ks[24], (L, D), 0.02),
    }


def reference(x, c, positions, w_in, w_o, hgrn_lower_bounds, hgrn_norm_w,
              cmp_pe_k, cmp_w1_k, cmp_w2_k, cmp_pe_v, cmp_w1_v, cmp_w2_v,
              gmlp_norm_w, gmlp_norm_b, gmlp_w_s, gmlp_b_s,
              w_ff1, w_ff2, w_ada, b_ada, ln1_w, ln1_b, ln2_w, ln2_b):
    B_, S_, D = x.shape
    inv_freq = ROPE_THETA ** (-jnp.arange(0, HEAD_DIM, 2, dtype=jnp.float32) / HEAD_DIM)
    ang = positions.astype(jnp.float32)[..., None] * inv_freq
    cos, sin = jnp.cos(ang)[:, :, None, :], jnp.sin(ang)[:, :, None, :]
    lb_sm = jax.nn.softmax(hgrn_lower_bounds.astype(jnp.float32), axis=0)
    lb_all = jnp.cumsum(lb_sm, axis=0) - lb_sm[0:1]
    sizes = [HG_W] * 4 + [NSA_W] + [KV_W] * 6 + [3 * NSA_HEADS] + [GM_W] * 2
    cuts = [int(v) for v in np.cumsum(sizes)[:-1]]
    ada_in = jax.nn.silu(c)
    for l in range(DEPTH):
        mod = (ada_in @ w_ada[l] + b_ada[l])[:, None, :]
        sh1, sc1, g1, sh2, sc2, g2 = jnp.split(mod, 6, axis=-1)
        h = x * (1.0 + sc1) + sh1
        proj = h @ w_in[l]
        (hq, hf, hi, hg, nq, kcm, vcm, ksl, vsl, kwn, vwn, ngt, gu, gv) = jnp.split(proj, cuts, axis=-1)
        hs = lambda a, n: a.reshape(B_, S_, n, HEAD_DIM)
        o_h = hgrn2(hs(hq, HG_HEADS), hs(hf, HG_HEADS), hs(hi, HG_HEADS), hs(hg, HG_HEADS),
                    lb_all[l], hgrn_norm_w[l])
        q_n = rope(hs(nq, NSA_HEADS), cos, sin)
        gates = jax.nn.sigmoid(ngt.astype(jnp.float32)).reshape(B_, S_, NSA_KV_HEADS, NSA_GQA, 3)
        o_n = nsa(q_n,
                  rope(hs(kcm, NSA_KV_HEADS), cos, sin), hs(vcm, NSA_KV_HEADS),
                  rope(hs(ksl, NSA_KV_HEADS), cos, sin), hs(vsl, NSA_KV_HEADS),
                  rope(hs(kwn, NSA_KV_HEADS), cos, sin), hs(vwn, NSA_KV_HEADS),
                  gates, cmp_pe_k[l], cmp_w1_k[l], cmp_w2_k[l], cmp_pe_v[l], cmp_w1_v[l], cmp_w2_v[l])
        o_g = gmlp(gu, gv, gmlp_norm_w[l], gmlp_norm_b[l], gmlp_w_s[l], gmlp_b_s[l])
        mix = jnp.concatenate([o_h, o_n, o_g], axis=-1) @ w_o[l]
        x = layer_norm(DN_ALPHA * x + (1.0 + g1) * mix, ln1_w[l], ln1_b[l])
        h = x * (1.0 + sc2) + sh2
        y = jnp.square(jax.nn.relu(h @ w_ff1[l])) @ w_ff2[l]
        x = layer_norm(DN_ALPHA * x + (1.0 + g2) * y, ln2_w[l], ln2_b[l])
    return x
```

```python
import functools

import numpy as np
import jax
import jax.numpy as jnp
from jax import lax
from jax.experimental import pallas as pl
from jax.experimental.pallas import tpu as pltpu

F32 = jnp.float32
BF16 = jnp.bfloat16

HEAD_DIM = 64
HG_HEADS = 4
HG_W = HG_HEADS * HEAD_DIM
NSA_HEADS = 8
NSA_W = NSA_HEADS * HEAD_DIM
NSA_GQA = 4
NSA_KV_HEADS = NSA_HEADS // NSA_GQA
KV_W = NSA_KV_HEADS * HEAD_DIM
GM_HEADS = 4
GM_W = GM_HEADS * HEAD_DIM
N_GATES = 3 * NSA_HEADS
ROPE_THETA = 10000.0
HGRN_CHUNK = 64
CMP_LEN = 32
CMP_STRIDE = 16
SLC_BLOCK = 64
SLC_TOPN = 16
WINDOW = 512
Q_BLOCK = 128
GMLP_CHUNK = 128
NEG = -1e30
BIG = 1e30
F_MIN = 1e-30
LANES = 128
ROW_TILE = 512
SLC_KEYS = 256
VMEM_LIMIT = 56 * 1024 * 1024


def _params(n_axes, semantics="parallel"):
    return pltpu.CompilerParams(dimension_semantics=(semantics,) * n_axes, vmem_limit_bytes=VMEM_LIMIT)


def _dot(a, b):
    return jnp.dot(a, b, preferred_element_type=F32)


def _dot_nt(a, b):
    return lax.dot_general(a, b, (((1,), (1,)), ((), ())), preferred_element_type=F32)


def _split2(x):
    hi = x.astype(BF16)
    lo = (x - hi.astype(F32)).astype(BF16)
    return hi, lo


def _split3(x):
    hi = x.astype(BF16)
    r = x - hi.astype(F32)
    mid = r.astype(BF16)
    lo = (r - mid.astype(F32)).astype(BF16)
    return hi, mid, lo


def _layer_norm(y, w, b):
    mu = jnp.mean(y, axis=-1, keepdims=True)
    d = y - mu
    var = jnp.mean(d * d, axis=-1, keepdims=True)
    return d * lax.rsqrt(var + 1e-5) * w + b


def _mod_kernel(c_ref, w_ref, b_ref, o_ref):
    a = c_ref[...]
    a = a * jax.nn.sigmoid(a)
    ah, al = _split2(a)
    wh, wl = _split2(w_ref[0])
    o_ref[0] = _dot(ah, wh) + _dot(al, wh) + _dot(ah, wl) + b_ref[0]


def _modulation(c8, w_ada, b_ada):
    L, D, N = w_ada.shape
    tn = 1024
    return pl.pallas_call(
        _mod_kernel,
        grid=(L, N // tn),
        in_specs=[pl.BlockSpec((8, D), lambda l, j: (0, 0)),
                  pl.BlockSpec((1, D, tn), lambda l, j: (l, 0, j)),
                  pl.BlockSpec((1, 1, tn), lambda l, j: (l, 0, j))],
        out_specs=pl.BlockSpec((1, 8, tn), lambda l, j: (l, 0, j)),
        out_shape=jax.ShapeDtypeStruct((L, 8, N), F32),
        compiler_params=_params(2),
        name="adaln_mod",
    )(c8, w_ada, b_ada.reshape(L, 1, N))


C_HG = 0
C_Q = C_HG + 4 * HG_W
C_K = C_Q + NSA_W
C_V = C_K + 3 * KV_W
C_GM = C_V + 3 * KV_W
C_GT = C_GM + 2 * GM_W
C_END = C_GT + LANES


def _in_col_index():
    o = {}
    off = 0
    for name, size in (("hq", HG_W), ("hf", HG_W), ("hi", HG_W), ("hg", HG_W), ("nq", NSA_W),
                       ("kcm", KV_W), ("vcm", KV_W), ("ksl", KV_W), ("vsl", KV_W), ("kwn", KV_W),
                       ("vwn", KV_W), ("ngt", N_GATES), ("gu", GM_W), ("gv", GM_W)):
        o[name] = np.arange(off, off + size)
        off += size
    idx = np.concatenate([o["hq"], o["hf"], o["hi"], o["hg"], o["nq"],
                          o["kcm"], o["ksl"], o["kwn"], o["vcm"], o["vsl"], o["vwn"],
                          o["gu"], o["gv"], o["ngt"], np.full(LANES - N_GATES, -1)])
    return idx, off


def _rope(a, cos_f, sin_s):
    lane = lax.broadcasted_iota(jnp.int32, a.shape, 1)
    first = (lane & (HEAD_DIM - 1)) < HEAD_DIM // 2
    partner = jnp.where(first, pltpu.roll(a, LANES - HEAD_DIM // 2, 1), pltpu.roll(a, HEAD_DIM // 2, 1))
    return a * cos_f + partner * sin_s


def _inproj_kernel(x_ref, sc_ref, sh_ref, w_ref, cos_ref, sin_ref,
                   hg_ref, q_ref, k_ref, v_ref, gm_ref, gt_ref):
    h = (x_ref[...] * (1.0 + sc_ref[0]) + sh_ref[0]).astype(BF16)
    cos_f = cos_ref[...]
    sin_s = sin_ref[...]
    hg_ref[...] = _dot(h, w_ref[:, C_HG:C_Q])
    for j in range(NSA_W // LANES):
        a = _dot(h, w_ref[:, C_Q + j * LANES:C_Q + (j + 1) * LANES])
        q_ref[:, j * LANES:(j + 1) * LANES] = (_rope(a, cos_f, sin_s) * (HEAD_DIM ** -0.5)).astype(BF16)
    for j in range(3):
        a = _dot(h, w_ref[:, C_K + j * LANES:C_K + (j + 1) * LANES])
        k_ref[:, j * LANES:(j + 1) * LANES] = _rope(a, cos_f, sin_s).astype(BF16)
    v_ref[...] = _dot(h, w_ref[:, C_V:C_GM]).astype(BF16)
    gm_ref[...] = _dot(h, w_ref[:, C_GM:C_GT])
    gt_ref[...] = jax.nn.sigmoid(_dot(h, w_ref[:, C_GT:C_END]))


def _in_projection(x2, sc, sh, w_cat, cos_f, sin_s, seq):
    M, D = x2.shape
    tm = ROW_TILE
    per_b = seq // tm
    row = lambda i: (i, 0)
    bmap = lambda i: (i // per_b, 0, 0)
    return pl.pallas_call(
        _inproj_kernel,
        grid=(M // tm,),
        in_specs=[pl.BlockSpec((tm, D), row),
                  pl.BlockSpec((1, 1, D), bmap),
                  pl.BlockSpec((1, 1, D), bmap),
                  pl.BlockSpec((D, C_END), lambda i: (0, 0)),
                  pl.BlockSpec((tm, LANES), row),
                  pl.BlockSpec((tm, LANES), row)],
        out_specs=[pl.BlockSpec((tm, 4 * HG_W), row),
                   pl.BlockSpec((tm, NSA_W), row),
                   pl.BlockSpec((tm, 3 * KV_W), row),
                   pl.BlockSpec((tm, 3 * KV_W), row),
                   pl.BlockSpec((tm, 2 * GM_W), row),
                   pl.BlockSpec((tm, LANES), row)],
        out_shape=[jax.ShapeDtypeStruct((M, 4 * HG_W), F32),
                   jax.ShapeDtypeStruct((M, NSA_W), BF16),
                   jax.ShapeDtypeStruct((M, 3 * KV_W), BF16),
                   jax.ShapeDtypeStruct((M, 3 * KV_W), BF16),
                   jax.ShapeDtypeStruct((M, 2 * GM_W), F32),
                   jax.ShapeDtypeStruct((M, LANES), F32)],
        compiler_params=_params(1),
        name="in_proj",
    )(x2, sc, sh, w_cat, cos_f, sin_s)


def _hgrn_kernel(x_ref, lb_ref, nw_ref, bd_ref, o_ref, st_ref, b_s, q_s, oi_s, *, steps_per_seq):
    C = HGRN_CHUNK
    W = HG_W

    @pl.when(pl.program_id(0) % steps_per_seq == 0)
    def _():
        st_ref[...] = jnp.zeros_like(st_ref)

    lb = lb_ref[...]
    nw = nw_ref[...]
    bd = bd_ref[...]
    bd_f = bd.astype(F32)
    row = lax.broadcasted_iota(jnp.int32, (C, W), 0)
    tri = (lax.broadcasted_iota(jnp.int32, (C, C), 0) >= lax.broadcasted_iota(jnp.int32, (C, C), 1)).astype(BF16)

    def chunk(ci, carry):
        r0 = pl.multiple_of(ci * C, C)
        q = x_ref[pl.ds(r0, C), 0:W]
        z = x_ref[pl.ds(r0, C), W:2 * W]
        v = x_ref[pl.ds(r0, C), 2 * W:3 * W]
        g = x_ref[pl.ds(r0, C), 3 * W:4 * W]
        f = lb + (1.0 - lb) * jax.nn.sigmoid(z)
        log_f = jnp.log(jnp.maximum(f, F_MIN))
        kk = (1.0 - lb) * jax.nn.sigmoid(-z)
        qf = q * jax.nn.sigmoid(q)
        l_hi, l_mid, l_lo = _split3(log_f)
        b = _dot(tri, l_hi) + _dot(tri, l_mid) + _dot(tri, l_lo)
        b_s[...] = b
        q_s[...] = qf

        def tok(t, c2):
            bt = b_s[pl.ds(t, 1), :]
            qt = q_s[pl.ds(t, 1), :]
            e = jnp.exp(jnp.where(row <= t, bt - b, NEG))
            p = (e * qt * kk).astype(BF16)
            a = _dot(p, bd)
            oi_s[pl.ds(t, 1), :] = jnp.sum(a * v, axis=0, keepdims=True)
            return c2

        lax.fori_loop(0, C, tok, 0)
        st = st_ref[...]
        o = oi_s[...] + _dot_nt((qf * jnp.exp(b)).astype(BF16), st.astype(BF16))
        b_last = b[C - 1:C, :]
        kd = (kk * jnp.exp(b_last - b)).astype(BF16)
        upd = lax.dot_general(v.astype(BF16), kd, (((0,), (0,)), ((), ())), preferred_element_type=F32)
        st_ref[...] = st * jnp.exp(b_last) + upd * bd_f
        oo = o * o
        o_hi, o_lo = _split2(oo)
        ms = (_dot(o_hi, bd) + _dot(o_lo, bd)) * (1.0 / HEAD_DIM)
        y = o * lax.rsqrt(ms + 1e-6) * nw
        o_ref[pl.ds(r0, C), :] = (y * (g * jax.nn.sigmoid(g))).astype(o_ref.dtype)
        return carry

    lax.fori_loop(0, x_ref.shape[0] // C, chunk, 0)


def _hgrn2(hg, lb, nw, bd, seq):
    M = hg.shape[0]
    tm = ROW_TILE
    return pl.pallas_call(
        functools.partial(_hgrn_kernel, steps_per_seq=seq // tm),
        grid=(M // tm,),
        in_specs=[pl.BlockSpec((tm, 4 * HG_W), lambda i: (i, 0)),
                  pl.BlockSpec((1, HG_W), lambda i: (0, 0)),
                  pl.BlockSpec((1, HG_W), lambda i: (0, 0)),
                  pl.BlockSpec((HG_W, HG_W), lambda i: (0, 0))],
        out_specs=pl.BlockSpec((tm, HG_W), lambda i: (i, 0)),
        out_shape=jax.ShapeDtypeStruct((M, HG_W), BF16),
        scratch_shapes=[pltpu.VMEM((HG_W, HG_W), F32),
                        pltpu.VMEM((HGRN_CHUNK, HG_W), F32),
                        pltpu.VMEM((HGRN_CHUNK, HG_W), F32),
                        pltpu.VMEM((HGRN_CHUNK, HG_W), F32)],
        compiler_params=_params(1, "arbitrary"),
        name="hgrn2",
    )(hg, lb, nw, bd)


def _gelu(x):
    return jax.nn.gelu(x)


def _gmlp_kernel(x_ref, nw_ref, nb_ref, ws_ref, bias_ref, o_ref):
    T = GMLP_CHUNK
    u = _gelu(x_ref[:, 0:GM_W])
    v = _layer_norm(_gelu(x_ref[:, GM_W:2 * GM_W]), nw_ref[...], nb_ref[...]).astype(BF16)
    lower = lax.broadcasted_iota(jnp.int32, (T, T), 0) >= lax.broadcasted_iota(jnp.int32, (T, T), 1)
    head = lax.broadcasted_iota(jnp.int32, (T, GM_W), 1) >> 6
    bias = bias_ref[...]
    for c in range(x_ref.shape[0] // T):
        vc = v[c * T:(c + 1) * T]
        sv = bias
        for g in range(GM_HEADS):
            w = jnp.where(lower, ws_ref[g], 0.0).astype(BF16)
            sv = sv + jnp.where(head == g, _dot(w, vc), 0.0)
        o_ref[c * T:(c + 1) * T, :] = (u[c * T:(c + 1) * T] * sv).astype(o_ref.dtype)


def _gmlp(gm, nw, nb, w_s, bias):
    M = gm.shape[0]
    tm = ROW_TILE
    T = GMLP_CHUNK
    return pl.pallas_call(
        _gmlp_kernel,
        grid=(M // tm,),
        in_specs=[pl.BlockSpec((tm, 2 * GM_W), lambda i: (i, 0)),
                  pl.BlockSpec((1, GM_W), lambda i: (0, 0)),
                  pl.BlockSpec((1, GM_W), lambda i: (0, 0)),
                  pl.BlockSpec((GM_HEADS, T, T), lambda i: (0, 0, 0)),
                  pl.BlockSpec((T, GM_W), lambda i: (0, 0))],
        out_specs=pl.BlockSpec((tm, GM_W), lambda i: (i, 0)),
        out_shape=jax.ShapeDtypeStruct((M, GM_W), BF16),
        compiler_params=_params(1),
        name="gmlp",
    )(gm, nw, nb, w_s, bias)


def _compress_kernel(a_ref, pe_ref, w1_ref, w2_ref, o_ref):
    half = w1_ref.shape[0] // 2
    a = a_ref[0]
    n_seg = a.shape[0]
    u = _dot(a, w1_ref[0:half, :])
    v = _dot(a, w1_ref[half:2 * half, :])
    pe_hi, pe_lo = _split2(pe_ref[...])
    c0 = (_dot(pe_hi, w1_ref[...]) + _dot(pe_lo, w1_ref[...]))[0:1, :]
    h1 = u + pltpu.roll(v, n_seg - 1, 0) + c0
    out = _dot(_gelu(h1).astype(BF16), w2_ref[...])
    valid = lax.broadcasted_iota(jnp.int32, out.shape, 0) < n_seg - 1
    o_ref[0] = jnp.where(valid, out, 0.0).astype(o_ref.dtype)


def _compress(a, pe8, w1, w2):
    N, n_seg, feat = a.shape
    return pl.pallas_call(
        _compress_kernel,
        grid=(N,),
        in_specs=[pl.BlockSpec((1, n_seg, feat), lambda i: (i, 0, 0)),
                  pl.BlockSpec(pe8.shape, lambda i: (0, 0)),
                  pl.BlockSpec(w1.shape, lambda i: (0, 0)),
                  pl.BlockSpec(w2.shape, lambda i: (0, 0))],
        out_specs=pl.BlockSpec((1, n_seg, HEAD_DIM), lambda i: (i, 0, 0)),
        out_shape=jax.ShapeDtypeStruct((N, n_seg, HEAD_DIM), BF16),
        compiler_params=_params(1),
        name="nsa_compress",
    )(a, pe8, w1, w2)


def _cmp_select_kernel(q_ref, kc_ref, vc_ref, ov_ref, o_ref, sel_ref, *, n_blocks):
    Q = Q_BLOCK
    n_cmp = kc_ref.shape[2]
    q0 = pl.program_id(1) * Q
    t = q0 + lax.broadcasted_iota(jnp.int32, (Q, 1), 0)
    cmp_end = lax.broadcasted_iota(jnp.int32, (Q, n_cmp), 1) * CMP_STRIDE + (CMP_LEN - 1)
    valid = cmp_end <= t
    j = lax.broadcasted_iota(jnp.int32, (Q, LANES), 1)
    j_f = j.astype(F32)
    tb = t >> 6
    forced = (j == 0) | (j == tb) | (j == tb - 1)
    ov = ov_ref[...]
    for hk in range(NSA_KV_HEADS):
        kc = kc_ref[0, hk]
        vc = vc_ref[0, hk]
        p_sum = jnp.zeros((Q, n_cmp), F32)
        for g in range(NSA_GQA):
            h = hk * NSA_GQA + g
            s = _dot_nt(q_ref[:, h * HEAD_DIM:(h + 1) * HEAD_DIM], kc)
            sm = jnp.where(valid, s, NEG)
            e = jnp.exp(sm - jnp.max(sm, axis=-1, keepdims=True))
            p = jnp.where(valid, e / jnp.sum(e, axis=-1, keepdims=True), 0.0)
            o_ref[:, h * HEAD_DIM:(h + 1) * HEAD_DIM] = _dot(p.astype(BF16), vc)
            p_sum = p_sum + p
        p_hi, p_lo = _split2(p_sum)
        imp = _dot(p_hi, ov) + _dot(p_lo, ov)
        score = jnp.where(j > tb, NEG, jnp.where(forced, BIG, imp))
        score = jnp.where(j < n_blocks, score, -jnp.inf)
        sel = jnp.zeros((Q, LANES), F32)
        for _ in range(min(SLC_TOPN, n_blocks)):
            top = jnp.max(score, axis=-1, keepdims=True)
            first = jnp.min(jnp.where(score == top, j_f, float(LANES)), axis=-1, keepdims=True)
            hit = j_f == first
            sel = jnp.where(hit, 1.0, sel)
            score = jnp.where(hit, -jnp.inf, score)
        sel_ref[0, hk] = sel.astype(sel_ref.dtype)


def _cmp_select(q, kc, vc, overlap, batch, seq):
    M = q.shape[0]
    Q = Q_BLOCK
    nq = seq // Q
    n_cmp = kc.shape[2]
    return pl.pallas_call(
        functools.partial(_cmp_select_kernel, n_blocks=seq // SLC_BLOCK),
        grid=(batch, nq),
        in_specs=[pl.BlockSpec((Q, NSA_W), lambda b, i: (b * nq + i, 0)),
                  pl.BlockSpec((1, NSA_KV_HEADS, n_cmp, HEAD_DIM), lambda b, i: (b, 0, 0, 0)),
                  pl.BlockSpec((1, NSA_KV_HEADS, n_cmp, HEAD_DIM), lambda b, i: (b, 0, 0, 0)),
                  pl.BlockSpec((n_cmp, LANES), lambda b, i: (0, 0))],
        out_specs=[pl.BlockSpec((Q, NSA_W), lambda b, i: (b * nq + i, 0)),
                   pl.BlockSpec((1, NSA_KV_HEADS, Q, LANES), lambda b, i: (b, 0, i, 0))],
        out_shape=[jax.ShapeDtypeStruct((M, NSA_W), F32),
                   jax.ShapeDtypeStruct((batch, NSA_KV_HEADS, seq, LANES), BF16)],
        compiler_params=_params(2),
        name="nsa_cmp_select",
    )(q, kc, vc, overlap)


def _attend(carry, qg, kch, vch, mask):
    m, l, acc = carry
    s = _dot_nt(qg, kch)
    sm = jnp.where(mask, s, NEG)
    m_new = jnp.maximum(m, jnp.max(sm, axis=-1, keepdims=True))
    p = jnp.where(mask, jnp.exp(sm - m_new), 0.0)
    alpha = jnp.exp(m - m_new)
    l = alpha * l + jnp.sum(p, axis=-1, keepdims=True)
    acc = alpha * acc + _dot(p.astype(BF16), vch)
    return m_new, l, acc


def _slc_win_kernel(q_ref, ks_ref, vs_ref, kw_ref, vw_ref, sel_ref, gt_ref, oc_ref, o_ref):
    Q = Q_BLOCK
    G = NSA_GQA
    KC = SLC_KEYS
    q0 = pl.program_id(1) * Q
    t = q0 + lax.broadcasted_iota(jnp.int32, (Q, 1), 0)
    init = tuple((jnp.full((Q, 1), NEG, F32), jnp.zeros((Q, 1), F32), jnp.zeros((Q, HEAD_DIM), F32))
                 for _ in range(G))
    n_slc = (q0 + Q + KC - 1) // KC
    blk_row = lax.broadcasted_iota(jnp.int32, (LANES, KC), 0)
    blk_col = lax.broadcasted_iota(jnp.int32, (LANES, KC), 1) >> 6
    key_off = lax.broadcasted_iota(jnp.int32, (Q, KC), 1)
    win_off = lax.broadcasted_iota(jnp.int32, (Q, Q), 1)
    gates = gt_ref[...]
    for hk in range(NSA_KV_HEADS):
        lanes = slice(hk * HEAD_DIM, (hk + 1) * HEAD_DIM)
        qs = [q_ref[:, (hk * G + g) * HEAD_DIM:(hk * G + g + 1) * HEAD_DIM] for g in range(G)]
        sel = sel_ref[0, hk]

        def slc_step(c, carry):
            k0 = pl.multiple_of(c * KC, KC)
            expand = (blk_row == blk_col + c * (KC // SLC_BLOCK)).astype(BF16)
            chosen = _dot(sel, expand)
            mask = jnp.where(k0 + key_off <= t, chosen, 0.0) > 0.5
            kch = ks_ref[pl.ds(k0, KC), lanes]
            vch = vs_ref[pl.ds(k0, KC), lanes]
            return tuple(_attend(carry[g], qs[g], kch, vch, mask) for g in range(G))

        slc = lax.fori_loop(0, n_slc, slc_step, init)

        win = init
        for w in range(WINDOW // Q + 1):
            kp0 = q0 - WINDOW + w * Q
            k0 = pl.multiple_of(jnp.maximum(kp0, 0), Q)
            kp = kp0 + win_off
            mask = jnp.where(kp <= t, jnp.where(kp > t - WINDOW, kp, -1), -1) >= 0
            kch = kw_ref[pl.ds(k0, Q), lanes]
            vch = vw_ref[pl.ds(k0, Q), lanes]
            win = tuple(_attend(win[g], qs[g], kch, vch, mask) for g in range(G))

        for g in range(G):
            h = hk * G + g
            out = (gates[:, 3 * h:3 * h + 1] * oc_ref[:, h * HEAD_DIM:(h + 1) * HEAD_DIM]
                   + gates[:, 3 * h + 1:3 * h + 2] * (slc[g][2] / slc[g][1])
                   + gates[:, 3 * h + 2:3 * h + 3] * (win[g][2] / win[g][1]))
            o_ref[:, h * HEAD_DIM:(h + 1) * HEAD_DIM] = out.astype(o_ref.dtype)


def _slc_win(q, k, v, sel, gates, o_cmp, batch, seq):
    M = q.shape[0]
    Q = Q_BLOCK
    nq = seq // Q
    qmap = lambda b, i: (b * nq + i, 0)
    return pl.pallas_call(
        _slc_win_kernel,
        grid=(batch, nq),
        in_specs=[pl.BlockSpec((Q, NSA_W), qmap),
                  pl.BlockSpec((seq, KV_W), lambda b, i: (b, 1)),
                  pl.BlockSpec((seq, KV_W), lambda b, i: (b, 1)),
                  pl.BlockSpec((seq, KV_W), lambda b, i: (b, 2)),
                  pl.BlockSpec((seq, KV_W), lambda b, i: (b, 2)),
                  pl.BlockSpec((1, NSA_KV_HEADS, Q, LANES), lambda b, i: (b, 0, i, 0)),
                  pl.BlockSpec((Q, LANES), qmap),
                  pl.BlockSpec((Q, NSA_W), qmap)],
        out_specs=pl.BlockSpec((Q, NSA_W), qmap),
        out_shape=jax.ShapeDtypeStruct((M, NSA_W), BF16),
        compiler_params=_params(2),
        name="nsa_slc_win",
    )(q, k, v, k, v, sel, gates, o_cmp)


def _outproj_kernel(oh_ref, on_ref, og_ref, w_ref, x_ref, g_ref, lw_ref, lb_ref, o_ref, *, alpha):
    mix = (_dot(oh_ref[...], w_ref[0:HG_W, :])
           + _dot(on_ref[...], w_ref[HG_W:HG_W + NSA_W, :])
           + _dot(og_ref[...], w_ref[HG_W + NSA_W:HG_W + NSA_W + GM_W, :]))
    y = alpha * x_ref[...] + (1.0 + g_ref[0]) * mix
    o_ref[...] = _layer_norm(y, lw_ref[...], lb_ref[...])


def _out_projection(o_h, o_n, o_g, w_o, x2, g1, ln_w, ln_b, seq, alpha):
    M, D = x2.shape
    tm = ROW_TILE
    per_b = seq // tm
    row = lambda i: (i, 0)
    const = lambda i: (0, 0)
    return pl.pallas_call(
        functools.partial(_outproj_kernel, alpha=alpha),
        grid=(M // tm,),
        in_specs=[pl.BlockSpec((tm, HG_W), row),
                  pl.BlockSpec((tm, NSA_W), row),
                  pl.BlockSpec((tm, GM_W), row),
                  pl.BlockSpec(w_o.shape, const),
                  pl.BlockSpec((tm, D), row),
                  pl.BlockSpec((1, 1, D), lambda i: (i // per_b, 0, 0)),
                  pl.BlockSpec((1, D), const),
                  pl.BlockSpec((1, D), const)],
        out_specs=pl.BlockSpec((tm, D), row),
        out_shape=jax.ShapeDtypeStruct((M, D), F32),
        compiler_params=_params(1),
        name="out_proj_ln",
    )(o_h, o_n, o_g, w_o, x2, g1, ln_w, ln_b)


def _ffn_kernel(x_ref, sc_ref, sh_ref, g_ref, w1_ref, w2_ref, lw_ref, lb_ref, o_ref, *, alpha, ff_chunk):
    x = x_ref[...]
    h = (x * (1.0 + sc_ref[0]) + sh_ref[0]).astype(BF16)
    y = jnp.zeros(x.shape, F32)
    for c in range(w1_ref.shape[1] // ff_chunk):
        a = jnp.maximum(_dot(h, w1_ref[:, c * ff_chunk:(c + 1) * ff_chunk]), 0.0)
        y = y + _dot((a * a).astype(BF16), w2_ref[c * ff_chunk:(c + 1) * ff_chunk, :])
    z = alpha * x + (1.0 + g_ref[0]) * y
    o_ref[...] = _layer_norm(z, lw_ref[...], lb_ref[...])


def _ffn(x2, sc, sh, g2, w1, w2, ln_w, ln_b, seq, alpha):
    M, D = x2.shape
    tm = ROW_TILE
    per_b = seq // tm
    row = lambda i: (i, 0)
    const = lambda i: (0, 0)
    bmap = lambda i: (i // per_b, 0, 0)
    return pl.pallas_call(
        functools.partial(_ffn_kernel, alpha=alpha, ff_chunk=1024),
        grid=(M // tm,),
        in_specs=[pl.BlockSpec((tm, D), row),
                  pl.BlockSpec((1, 1, D), bmap),
                  pl.BlockSpec((1, 1, D), bmap),
                  pl.BlockSpec((1, 1, D), bmap),
                  pl.BlockSpec(w1.shape, const),
                  pl.BlockSpec(w2.shape, const),
                  pl.BlockSpec((1, D), const),
                  pl.BlockSpec((1, D), const)],
        out_specs=pl.BlockSpec((tm, D), row),
        out_shape=jax.ShapeDtypeStruct((M, D), F32),
        compiler_params=_params(1),
        name="ffn_ln",
    )(x2, sc, sh, g2, w1, w2, ln_w, ln_b)


def _segments(a, batch, seq):
    a = a.reshape(batch, seq // CMP_STRIDE, CMP_STRIDE, NSA_KV_HEADS, HEAD_DIM)
    return a.transpose(0, 3, 1, 2, 4).reshape(batch * NSA_KV_HEADS, seq // CMP_STRIDE, CMP_STRIDE * HEAD_DIM)


def kernel(x, c, positions, w_in, w_o, hgrn_lower_bounds, hgrn_norm_w, cmp_pe_k, cmp_w1_k, cmp_w2_k, cmp_pe_v, cmp_w1_v, cmp_w2_v, gmlp_norm_w, gmlp_norm_b, gmlp_w_s, gmlp_b_s, w_ff1, w_ff2, w_ada, b_ada, ln1_w, ln1_b, ln2_w, ln2_b):
    B, S, D = x.shape
    L = w_in.shape[0]
    M = B * S
    alpha = (2 * L) ** 0.25
    n_seg = S // CMP_STRIDE

    inv_freq = ROPE_THETA ** (-jnp.arange(0, HEAD_DIM, 2, dtype=F32) / HEAD_DIM)
    ang = positions.astype(F32)[..., None] * inv_freq
    cos, sin = jnp.cos(ang).reshape(M, HEAD_DIM // 2), jnp.sin(ang).reshape(M, HEAD_DIM // 2)
    cos_f = jnp.concatenate([cos, cos, cos, cos], axis=-1)
    sin_s = jnp.concatenate([-sin, sin, -sin, sin], axis=-1)

    lb_sm = jax.nn.softmax(hgrn_lower_bounds.astype(F32), axis=0)
    lb_all = jnp.cumsum(lb_sm, axis=0) - lb_sm[0:1]

    col_idx, n_cols = _in_col_index()
    w_cat = jnp.where(jnp.asarray(col_idx >= 0)[None, None, :],
                      jnp.take(w_in, jnp.asarray(np.maximum(col_idx, 0)), axis=2), 0.0).astype(BF16)
    assert n_cols == w_in.shape[2]

    lane_head = np.arange(HG_W) // HEAD_DIM
    bd = jnp.asarray((lane_head[:, None] == lane_head[None, :]).astype(np.float32), dtype=BF16)
    cs = np.arange(n_seg) * CMP_STRIDE
    ss = np.arange(LANES) * SLC_BLOCK
    ov = (cs[:, None] < ss[None, :] + SLC_BLOCK) & (ss[None, :] < cs[:, None] + CMP_LEN)
    ov &= (np.arange(n_seg) < n_seg - 1)[:, None] & (np.arange(LANES) < S // SLC_BLOCK)[None, :]
    overlap = jnp.asarray(ov.astype(np.float32), dtype=BF16)

    c8 = jnp.pad(c, ((0, 8 - B), (0, 0)))
    mod = _modulation(c8, w_ada, b_ada)[:, :B]

    x2 = x.reshape(M, D)
    for l in range(L):
        sh1, sc1, g1, sh2, sc2, g2 = [mod[l, :, i * D:(i + 1) * D].reshape(B, 1, D) for i in range(6)]
        hg, q, k, v, gm, gates = _in_projection(x2, sc1, sh1, w_cat[l], cos_f, sin_s, S)

        o_h = _hgrn2(hg, lb_all[l].reshape(1, HG_W), jnp.tile(hgrn_norm_w[l], HG_HEADS).reshape(1, HG_W), bd, S)

        bias = jnp.repeat(gmlp_b_s[l].T, HEAD_DIM, axis=1)
        o_g = _gmlp(gm, gmlp_norm_w[l].reshape(1, GM_W), gmlp_norm_b[l].reshape(1, GM_W), gmlp_w_s[l], bias)

        pe_k = jnp.pad(cmp_pe_k[l].reshape(1, -1), ((0, 7), (0, 0)))
        pe_v = jnp.pad(cmp_pe_v[l].reshape(1, -1), ((0, 7), (0, 0)))
        kc = _compress(_segments(k[:, 0:KV_W], B, S), pe_k, cmp_w1_k[l].astype(BF16), cmp_w2_k[l].astype(BF16))
        vc = _compress(_segments(v[:, 0:KV_W], B, S), pe_v, cmp_w1_v[l].astype(BF16), cmp_w2_v[l].astype(BF16))
        kc = kc.reshape(B, NSA_KV_HEADS, n_seg, HEAD_DIM)
        vc = vc.reshape(B, NSA_KV_HEADS, n_seg, HEAD_DIM)
        o_cmp, sel = _cmp_select(q, kc, vc, overlap, B, S)
        o_n = _slc_win(q, k, v, sel, gates, o_cmp, B, S)

        x2 = _out_projection(o_h, o_n, o_g, w_o[l].astype(BF16), x2, g1,
                             ln1_w[l].reshape(1, D), ln1_b[l].reshape(1, D), S, alpha)
        x2 = _ffn(x2, sc2, sh2, g2, w_ff1[l].astype(BF16), w_ff2[l].astype(BF16),
                  ln2_w[l].reshape(1, D), ln2_b[l].reshape(1, D), S, alpha)
    return x2.reshape(B, S, D)
```

```python
import functools

import numpy as np
import jax
import jax.numpy as jnp
from jax import lax
from jax.experimental import pallas as pl
from jax.experimental.pallas import tpu as pltpu

F32 = jnp.float32
BF16 = jnp.bfloat16

HEAD_DIM = 64
HG_HEADS = 4
HG_W = HG_HEADS * HEAD_DIM
NSA_HEADS = 8
NSA_W = NSA_HEADS * HEAD_DIM
NSA_GQA = 4
NSA_KV_HEADS = NSA_HEADS // NSA_GQA
KV_W = NSA_KV_HEADS * HEAD_DIM
GM_HEADS = 4
GM_W = GM_HEADS * HEAD_DIM
N_GATES = 3 * NSA_HEADS
ROPE_THETA = 10000.0
HGRN_CHUNK = 64
CMP_LEN = 32
CMP_STRIDE = 16
SLC_BLOCK = 64
SLC_TOPN = 16
WINDOW = 512
Q_BLOCK = 128
GMLP_CHUNK = 128
NEG = -1e30
BIG = 1e30
F_MIN = 1e-30
LANES = 128
ROW_TILE = 512
SLC_KEYS = 512
Q_SCALE = HEAD_DIM ** -0.5 * 1.4426950408889634
VMEM_LIMIT = 56 * 1024 * 1024


def _params(n_axes, semantics="parallel"):
    return pltpu.CompilerParams(dimension_semantics=(semantics,) * n_axes, vmem_limit_bytes=VMEM_LIMIT)


def _dot(a, b):
    return jnp.dot(a, b, preferred_element_type=F32)


def _dot_nt(a, b):
    return lax.dot_general(a, b, (((1,), (1,)), ((), ())), preferred_element_type=F32)


def _split2(x):
    hi = x.astype(BF16)
    lo = (x - hi.astype(F32)).astype(BF16)
    return hi, lo


def _split3(x):
    hi = x.astype(BF16)
    r = x - hi.astype(F32)
    mid = r.astype(BF16)
    lo = (r - mid.astype(F32)).astype(BF16)
    return hi, mid, lo


def _tile_lanes(x, n):
    return x if n == 1 else jnp.concatenate([x] * n, axis=1)


def _row_max(s):
    return jnp.broadcast_to(jnp.max(s, axis=-1, keepdims=True), (s.shape[0], LANES))


def _row_sum(s):
    return jnp.broadcast_to(jnp.sum(s, axis=-1, keepdims=True), (s.shape[0], LANES))


def _layer_norm(y, w, b):
    mu = jnp.mean(y, axis=-1, keepdims=True)
    d = y - mu
    var = jnp.mean(d * d, axis=-1, keepdims=True)
    return d * lax.rsqrt(var + 1e-5) * w + b


def _mod_kernel(c_ref, w_ref, b_ref, o_ref):
    a = c_ref[...]
    a = a * jax.nn.sigmoid(a)
    ah, al = _split2(a)
    wh, wl = _split2(w_ref[0])
    o_ref[0] = _dot(ah, wh) + _dot(al, wh) + _dot(ah, wl) + b_ref[0]


def _modulation(c8, w_ada, b_ada):
    L, D, N = w_ada.shape
    tn = 1024
    return pl.pallas_call(
        _mod_kernel,
        grid=(L, N // tn),
        in_specs=[pl.BlockSpec((8, D), lambda l, j: (0, 0)),
                  pl.BlockSpec((1, D, tn), lambda l, j: (l, 0, j)),
                  pl.BlockSpec((1, 1, tn), lambda l, j: (l, 0, j))],
        out_specs=pl.BlockSpec((1, 8, tn), lambda l, j: (l, 0, j)),
        out_shape=jax.ShapeDtypeStruct((L, 8, N), F32),
        compiler_params=_params(2),
        name="adaln_mod",
    )(c8, w_ada, b_ada.reshape(L, 1, N))


C_HG = 0
C_Q = C_HG + 4 * HG_W
C_K = C_Q + NSA_W
C_V = C_K + 3 * KV_W
C_GM = C_V + 3 * KV_W
C_GT = C_GM + 2 * GM_W
C_END = C_GT + LANES


def _nsa_head_perm():
    heads = [hk * NSA_GQA + g for g in range(NSA_GQA) for hk in range(NSA_KV_HEADS)]
    return np.concatenate([np.arange(h * HEAD_DIM, (h + 1) * HEAD_DIM) for h in heads])


def _in_col_index():
    o = {}
    off = 0
    for name, size in (("hq", HG_W), ("hf", HG_W), ("hi", HG_W), ("hg", HG_W), ("nq", NSA_W),
                       ("kcm", KV_W), ("vcm", KV_W), ("ksl", KV_W), ("vsl", KV_W), ("kwn", KV_W),
                       ("vwn", KV_W), ("ngt", N_GATES), ("gu", GM_W), ("gv", GM_W)):
        o[name] = np.arange(off, off + size)
        off += size
    idx = np.concatenate([o["hq"], o["hf"], o["hi"], o["hg"], o["nq"][_nsa_head_perm()],
                          o["kcm"], o["ksl"], o["kwn"], o["vcm"], o["vsl"], o["vwn"],
                          o["gu"], o["gv"], o["ngt"], np.full(LANES - N_GATES, -1)])
    return idx, off


def _rope(a, cos_f, sin_s):
    lane = lax.broadcasted_iota(jnp.int32, a.shape, 1)
    first = (lane & (HEAD_DIM - 1)) < HEAD_DIM // 2
    partner = jnp.where(first, pltpu.roll(a, LANES - HEAD_DIM // 2, 1), pltpu.roll(a, HEAD_DIM // 2, 1))
    return a * cos_f + partner * sin_s


def _inproj_kernel(x_ref, sc_ref, sh_ref, w_ref, cos_ref, sin_ref,
                   hg_ref, q_ref, k_ref, v_ref, gm_ref, gt_ref):
    h = (x_ref[...] * (1.0 + sc_ref[0]) + sh_ref[0]).astype(BF16)
    cos_f = cos_ref[...]
    sin_s = sin_ref[...]
    hg_ref[...] = _dot(h, w_ref[:, C_HG:C_Q])
    for j in range(NSA_W // LANES):
        a = _dot(h, w_ref[:, C_Q + j * LANES:C_Q + (j + 1) * LANES])
        q_ref[:, j * LANES:(j + 1) * LANES] = (_rope(a, cos_f, sin_s) * Q_SCALE).astype(BF16)
    for j in range(3):
        a = _dot(h, w_ref[:, C_K + j * LANES:C_K + (j + 1) * LANES])
        k_ref[:, j * LANES:(j + 1) * LANES] = _rope(a, cos_f, sin_s).astype(BF16)
    v_ref[...] = _dot(h, w_ref[:, C_V:C_GM]).astype(BF16)
    gm_ref[...] = _dot(h, w_ref[:, C_GM:C_GT])
    gt_ref[...] = jax.nn.sigmoid(_dot(h, w_ref[:, C_GT:C_END]))


def _in_projection(x2, sc, sh, w_cat, cos_f, sin_s, seq):
    M, D = x2.shape
    tm = ROW_TILE
    per_b = seq // tm
    row = lambda i: (i, 0)
    bmap = lambda i: (i // per_b, 0, 0)
    return pl.pallas_call(
        _inproj_kernel,
        grid=(M // tm,),
        in_specs=[pl.BlockSpec((tm, D), row),
                  pl.BlockSpec((1, 1, D), bmap),
                  pl.BlockSpec((1, 1, D), bmap),
                  pl.BlockSpec((D, C_END), lambda i: (0, 0)),
                  pl.BlockSpec((tm, LANES), row),
                  pl.BlockSpec((tm, LANES), row)],
        out_specs=[pl.BlockSpec((tm, 4 * HG_W), row),
                   pl.BlockSpec((tm, NSA_W), row),
                   pl.BlockSpec((tm, 3 * KV_W), row),
                   pl.BlockSpec((tm, 3 * KV_W), row),
                   pl.BlockSpec((tm, 2 * GM_W), row),
                   pl.BlockSpec((tm, LANES), row)],
        out_shape=[jax.ShapeDtypeStruct((M, 4 * HG_W), F32),
                   jax.ShapeDtypeStruct((M, NSA_W), BF16),
                   jax.ShapeDtypeStruct((M, 3 * KV_W), BF16),
                   jax.ShapeDtypeStruct((M, 3 * KV_W), BF16),
                   jax.ShapeDtypeStruct((M, 2 * GM_W), F32),
                   jax.ShapeDtypeStruct((M, LANES), F32)],
        compiler_params=_params(1),
        name="in_proj",
    )(x2, sc, sh, w_cat, cos_f, sin_s)


HGRN_LEVELS = 6


def _hgrn_tables():
    C = HGRN_CHUNK
    r = np.arange(C)
    wins = [r[None, :] <= r[:, None], r[None, :] > r[:, None]]
    masks = []
    for lvl in range(HGRN_LEVELS):
        c = C >> (lvl + 1)
        blk = r // c
        pair = blk // 2
        upper = blk % 2 == 1
        bound = (2 * pair + 1) * c - 1
        win_u = (r[None, :] > bound[:, None]) & (r[None, :] <= r[:, None])
        win_l = (r[None, :] > r[:, None]) & (r[None, :] <= bound[:, None])
        wins.append(np.where(upper[:, None], win_u, win_l))
        masks.append(upper[:, None] & ~upper[None, :] & (pair[:, None] == pair[None, :]))
    masks.append(np.eye(C, dtype=bool))
    window = np.concatenate(wins, axis=0).astype(np.float32)
    pair_mask = np.stack([np.tile(m, (1, HG_HEADS)) for m in masks]).astype(np.float32)
    return window, pair_mask


def _hgrn_kernel(x_ref, lb_ref, nw_ref, bd_ref, win_ref, pm_ref, o_ref, st_ref, *, steps_per_seq):
    C = HGRN_CHUNK
    W = HG_W

    @pl.when(pl.program_id(0) % steps_per_seq == 0)
    def _():
        st_ref[...] = jnp.zeros_like(st_ref)

    lb = lb_ref[...]
    nw = nw_ref[...]
    bd = bd_ref[...]
    bd_f = bd.astype(F32)
    lane_head = lax.broadcasted_iota(jnp.int32, (C, W), 1) >> 6
    own = [lane_head == h for h in range(HG_HEADS)]

    def stack(a):
        return jnp.concatenate([jnp.where(own[h], a, 0.0) for h in range(HG_HEADS)], axis=0).astype(BF16)

    def chunk(ci, carry):
        r0 = pl.multiple_of(ci * C, C)
        q = x_ref[pl.ds(r0, C), 0:W]
        z = x_ref[pl.ds(r0, C), W:2 * W]
        v = x_ref[pl.ds(r0, C), 2 * W:3 * W]
        g = x_ref[pl.ds(r0, C), 3 * W:4 * W]
        f = lb + (1.0 - lb) * jax.nn.sigmoid(z)
        log_f = jnp.log(jnp.maximum(f, F_MIN))
        kk = (1.0 - lb) * jax.nn.sigmoid(-z)
        qf = q * jax.nn.sigmoid(q)
        l_hi, l_lo = _split2(log_f)
        xw = _dot(win_ref[...], jnp.concatenate([l_hi, l_lo], axis=1))
        e = jnp.exp(xw[:, 0:W] + xw[:, W:2 * W])
        e_b = e[0:C]
        e_suffix = e[C:2 * C]
        attn = _dot_nt(qf.astype(BF16), stack(kk)) * pm_ref[HGRN_LEVELS]
        for lvl in range(HGRN_LEVELS):
            e_l = e[(2 + lvl) * C:(3 + lvl) * C]
            attn = attn + _dot_nt((qf * e_l).astype(BF16), stack(kk * e_l)) * pm_ref[lvl]
        st = st_ref[...]
        o = _dot(attn.astype(BF16), stack(v)) + _dot_nt((qf * e_b).astype(BF16), st.astype(BF16))
        kd = (kk * e_suffix).astype(BF16)
        upd = lax.dot_general(v.astype(BF16), kd, (((0,), (0,)), ((), ())), preferred_element_type=F32)
        st_ref[...] = st * e_b[C - 1:C, :] + upd * bd_f
        oo = o * o
        o_hi, o_lo = _split2(oo)
        ms = (_dot(o_hi, bd) + _dot(o_lo, bd)) * (1.0 / HEAD_DIM)
        y = o * lax.rsqrt(ms + 1e-6) * nw
        o_ref[pl.ds(r0, C), :] = (y * (g * jax.nn.sigmoid(g))).astype(o_ref.dtype)
        return carry

    lax.fori_loop(0, x_ref.shape[0] // C, chunk, 0)


def _hgrn2(hg, lb, nw, bd, window, pair_mask, seq):
    M = hg.shape[0]
    tm = ROW_TILE
    return pl.pallas_call(
        functools.partial(_hgrn_kernel, steps_per_seq=seq // tm),
        grid=(M // tm,),
        in_specs=[pl.BlockSpec((tm, 4 * HG_W), lambda i: (i, 0)),
                  pl.BlockSpec((1, HG_W), lambda i: (0, 0)),
                  pl.BlockSpec((1, HG_W), lambda i: (0, 0)),
                  pl.BlockSpec((HG_W, HG_W), lambda i: (0, 0)),
                  pl.BlockSpec(window.shape, lambda i: (0, 0)),
                  pl.BlockSpec(pair_mask.shape, lambda i: (0, 0, 0))],
        out_specs=pl.BlockSpec((tm, HG_W), lambda i: (i, 0)),
        out_shape=jax.ShapeDtypeStruct((M, HG_W), BF16),
        scratch_shapes=[pltpu.VMEM((HG_W, HG_W), F32)],
        compiler_params=_params(1, "arbitrary"),
        name="hgrn2",
    )(hg, lb, nw, bd, window, pair_mask)


def _gelu(x):
    return jax.nn.gelu(x)


def _gmlp_kernel(x_ref, nw_ref, nb_ref, ws_ref, bias_ref, o_ref):
    T = GMLP_CHUNK
    u = _gelu(x_ref[:, 0:GM_W])
    v = _layer_norm(_gelu(x_ref[:, GM_W:2 * GM_W]), nw_ref[...], nb_ref[...]).astype(BF16)
    lower = lax.broadcasted_iota(jnp.int32, (T, T), 0) >= lax.broadcasted_iota(jnp.int32, (T, T), 1)
    head = lax.broadcasted_iota(jnp.int32, (T, GM_W), 1) >> 6
    bias = bias_ref[...]
    for c in range(x_ref.shape[0] // T):
        vc = v[c * T:(c + 1) * T]
        sv = bias
        for g in range(GM_HEADS):
            w = jnp.where(lower, ws_ref[g], 0.0).astype(BF16)
            sv = sv + jnp.where(head == g, _dot(w, vc), 0.0)
        o_ref[c * T:(c + 1) * T, :] = (u[c * T:(c + 1) * T] * sv).astype(o_ref.dtype)


def _gmlp(gm, nw, nb, w_s, bias):
    M = gm.shape[0]
    tm = ROW_TILE
    T = GMLP_CHUNK
    return pl.pallas_call(
        _gmlp_kernel,
        grid=(M // tm,),
        in_specs=[pl.BlockSpec((tm, 2 * GM_W), lambda i: (i, 0)),
                  pl.BlockSpec((1, GM_W), lambda i: (0, 0)),
                  pl.BlockSpec((1, GM_W), lambda i: (0, 0)),
                  pl.BlockSpec((GM_HEADS, T, T), lambda i: (0, 0, 0)),
                  pl.BlockSpec((T, GM_W), lambda i: (0, 0))],
        out_specs=pl.BlockSpec((tm, GM_W), lambda i: (i, 0)),
        out_shape=jax.ShapeDtypeStruct((M, GM_W), BF16),
        compiler_params=_params(1),
        name="gmlp",
    )(gm, nw, nb, w_s, bias)


def _compress_kernel(a_ref, pe_ref, w1_ref, w2_ref, o_ref):
    half = w1_ref.shape[0] // 2
    a = a_ref[0]
    n_seg = a.shape[0]
    u = _dot(a, w1_ref[0:half, :])
    v = _dot(a, w1_ref[half:2 * half, :])
    pe_hi, pe_lo = _split2(pe_ref[...])
    c0 = (_dot(pe_hi, w1_ref[...]) + _dot(pe_lo, w1_ref[...]))[0:1, :]
    h1 = u + pltpu.roll(v, n_seg - 1, 0) + c0
    out = _dot(_gelu(h1).astype(BF16), w2_ref[...])
    valid = lax.broadcasted_iota(jnp.int32, out.shape, 0) < n_seg - 1
    o_ref[0] = jnp.where(valid, out, 0.0).astype(o_ref.dtype)


def _compress(a, pe8, w1, w2):
    N, n_seg, feat = a.shape
    return pl.pallas_call(
        _compress_kernel,
        grid=(N,),
        in_specs=[pl.BlockSpec((1, n_seg, feat), lambda i: (i, 0, 0)),
                  pl.BlockSpec(pe8.shape, lambda i: (0, 0)),
                  pl.BlockSpec(w1.shape, lambda i: (0, 0)),
                  pl.BlockSpec(w2.shape, lambda i: (0, 0))],
        out_specs=pl.BlockSpec((1, n_seg, HEAD_DIM), lambda i: (i, 0, 0)),
        out_shape=jax.ShapeDtypeStruct((N, n_seg, HEAD_DIM), BF16),
        compiler_params=_params(1),
        name="nsa_compress",
    )(a, pe8, w1, w2)


def _head_lanes(hk, g):
    return slice(g * LANES + hk * HEAD_DIM, g * LANES + (hk + 1) * HEAD_DIM)


def _cmp_select_kernel(q_ref, kc_ref, vc_ref, ov_ref, o_ref, sel_ref, *, n_blocks):
    Q = Q_BLOCK
    n_cmp = kc_ref.shape[2]
    q0 = pl.program_id(1) * Q
    t = q0 + lax.broadcasted_iota(jnp.int32, (Q, 1), 0)
    cmp_end = lax.broadcasted_iota(jnp.int32, (Q, n_cmp), 1) * CMP_STRIDE + (CMP_LEN - 1)
    valid = cmp_end <= t
    j = lax.broadcasted_iota(jnp.int32, (LANES, Q), 0)
    j_f = j.astype(F32)
    tb = (q0 + lax.broadcasted_iota(jnp.int32, (LANES, Q), 1)) >> 6
    forced = (j == 0) | (j == tb) | (j == tb - 1)
    ov_t = ov_ref[...]
    for hk in range(NSA_KV_HEADS):
        kc = kc_ref[0, hk]
        vc = vc_ref[0, hk]
        p_sum = jnp.zeros((Q, n_cmp), F32)
        for g in range(NSA_GQA):
            lanes = _head_lanes(hk, g)
            s = _dot_nt(q_ref[:, lanes], kc)
            sm = jnp.where(valid, s, NEG)
            e = jnp.exp2(sm - _tile_lanes(_row_max(sm), n_cmp // LANES))
            p = jnp.where(valid, e * _tile_lanes(1.0 / _row_sum(e), n_cmp // LANES), 0.0)
            o_ref[:, lanes] = _dot(p.astype(BF16), vc)
            p_sum = p_sum + p
        p_hi, p_lo = _split2(p_sum)
        imp = _dot_nt(ov_t, p_hi) + _dot_nt(ov_t, p_lo)
        score = jnp.where(j > tb, NEG, jnp.where(forced, BIG, imp))
        score = jnp.where(j < n_blocks, score, -jnp.inf)
        sel = jnp.zeros((LANES, Q), F32)
        for _ in range(min(SLC_TOPN, n_blocks)):
            top = jnp.max(score, axis=0, keepdims=True)
            first = jnp.min(jnp.where(score == top, j_f, float(LANES)), axis=0, keepdims=True)
            hit = j_f == first
            sel = jnp.where(hit, 1.0, sel)
            score = jnp.where(hit, -jnp.inf, score)
        sel_ref[0, hk] = jnp.where(sel > 0.5, 0.0, NEG).T.astype(sel_ref.dtype)


def _cmp_select(q, kc, vc, overlap, batch, seq):
    M = q.shape[0]
    Q = Q_BLOCK
    nq = seq // Q
    n_cmp = kc.shape[2]
    return pl.pallas_call(
        functools.partial(_cmp_select_kernel, n_blocks=seq // SLC_BLOCK),
        grid=(batch, nq),
        in_specs=[pl.BlockSpec((Q, NSA_W), lambda b, i: (b * nq + i, 0)),
                  pl.BlockSpec((1, NSA_KV_HEADS, n_cmp, HEAD_DIM), lambda b, i: (b, 0, 0, 0)),
                  pl.BlockSpec((1, NSA_KV_HEADS, n_cmp, HEAD_DIM), lambda b, i: (b, 0, 0, 0)),
                  pl.BlockSpec((LANES, n_cmp), lambda b, i: (0, 0))],
        out_specs=[pl.BlockSpec((Q, NSA_W), lambda b, i: (b * nq + i, 0)),
                   pl.BlockSpec((1, NSA_KV_HEADS, Q, LANES), lambda b, i: (b, 0, i, 0))],
        out_shape=[jax.ShapeDtypeStruct((M, NSA_W), F32),
                   jax.ShapeDtypeStruct((batch, NSA_KV_HEADS, seq, LANES), BF16)],
        compiler_params=_params(2),
        name="nsa_cmp_select",
    )(q, kc, vc, overlap)


def _window_bias():
    r = np.arange(Q_BLOCK)[:, None]
    j = np.arange(WINDOW + Q_BLOCK)[None, :]
    band = (j > r) & (j <= r + WINDOW)
    tabs = [np.where(band & (j >= WINDOW - Q_BLOCK * v), 0.0, NEG) for v in range(WINDOW // Q_BLOCK + 1)]
    return np.stack(tabs).astype(np.float32)


def _slc_win_kernel(q_ref, ks_ref, vs_ref, kw_ref, vw_ref, oh_ref, wb_ref, sel_ref, gt_ref, oc_ref, o_ref,
                    m_s, acc_s):
    Q = Q_BLOCK
    G = NSA_GQA
    HK = NSA_KV_HEADS
    R = HK * G * Q
    KC = SLC_KEYS
    NW = WINDOW // Q + 1
    q0 = pl.program_id(1) * Q
    n_full = q0 // KC
    t_row = q0 + (lax.broadcasted_iota(jnp.int32, (R, 1), 0) & (Q - 1))
    key_off = lax.broadcasted_iota(jnp.int32, (R, KC), 1)
    half_q = lax.broadcasted_iota(jnp.int32, (Q, LANES), 1) >> 6
    zero = jnp.zeros((), BF16)
    qpad = jnp.concatenate([jnp.where(half_q == hk, q_ref[:, g * LANES:(g + 1) * LANES], zero)
                            for hk in range(HK) for g in range(G)], axis=0)
    qaug = jnp.concatenate([qpad, jnp.concatenate([sel_ref[0, hk] for hk in range(HK) for _ in range(G)], axis=0)],
                           axis=1)
    m_s[...] = jnp.full(m_s.shape, NEG, F32)
    acc_s[...] = jnp.zeros(acc_s.shape, F32)

    def slc_chunk(c, causal):
        k0 = pl.multiple_of(c * KC, KC)
        kaug = jnp.concatenate([ks_ref[pl.ds(k0, KC), :], oh_ref[pl.ds(k0, KC), :]], axis=1)
        vaug = jnp.concatenate([vs_ref[pl.ds(k0, KC), :], jnp.ones((KC, LANES), BF16)], axis=1)
        s = _dot_nt(qaug, kaug)
        if causal:
            s = jnp.where(k0 + key_off <= t_row, s, NEG)
        m_old = m_s[...]
        m_new = jnp.maximum(m_old, jnp.max(s, axis=-1, keepdims=True))
        p = jnp.exp2(s - _tile_lanes(m_new, KC // LANES)).astype(BF16)
        acc_s[...] = _tile_lanes(jnp.exp2(m_old - m_new), 2) * acc_s[...] + _dot(p, vaug)
        m_s[...] = m_new

    def body(c, carry):
        slc_chunk(c, False)
        return carry

    lax.fori_loop(0, n_full, body, 0)
    slc_chunk(n_full, True)
    acc = acc_s[...]
    o_slc = acc[:, 0:LANES] / acc[:, LANES:2 * LANES]

    rows_k, rows_v = [], []
    for w in range(NW):
        k0 = pl.multiple_of(jnp.maximum(q0 - WINDOW + w * Q, 0), Q)
        rows_k.append(kw_ref[pl.ds(k0, Q), :])
        rows_v.append(vw_ref[pl.ds(k0, Q), :])
    k_win = jnp.concatenate(rows_k, axis=0)
    v_win = jnp.concatenate([jnp.concatenate(rows_v, axis=0), jnp.ones((NW * Q, LANES), BF16)], axis=1)
    s = _dot_nt(qpad, k_win) + jnp.concatenate([wb_ref[0]] * (HK * G), axis=0)
    p = jnp.exp2(s - _tile_lanes(_row_max(s), NW)).astype(BF16)
    acc = _dot(p, v_win)
    o_win = acc[:, 0:LANES] / acc[:, LANES:2 * LANES]

    gates = gt_ref[...]
    for g in range(G):
        oc = oc_ref[:, g * LANES:(g + 1) * LANES]
        tiles = []
        for hk in range(HK):
            c0 = 3 * (hk * G + g)
            rows = slice((hk * G + g) * Q, (hk * G + g + 1) * Q)
            tiles.append(gates[:, c0:c0 + 1] * oc + gates[:, c0 + 1:c0 + 2] * o_slc[rows]
                         + gates[:, c0 + 2:c0 + 3] * o_win[rows])
        o_ref[:, g * LANES:(g + 1) * LANES] = jnp.where(half_q == 0, tiles[0], tiles[1]).astype(o_ref.dtype)


def _slc_win(q, k, v, onehot, wbias, sel, gates, o_cmp, batch, seq):
    M = q.shape[0]
    Q = Q_BLOCK
    G = NSA_GQA
    nq = seq // Q
    n_var = wbias.shape[0]
    qmap = lambda b, i: (b * nq + i, 0)
    return pl.pallas_call(
        _slc_win_kernel,
        grid=(batch, nq),
        in_specs=[pl.BlockSpec((Q, NSA_W), qmap),
                  pl.BlockSpec((seq, KV_W), lambda b, i: (b, 1)),
                  pl.BlockSpec((seq, KV_W), lambda b, i: (b, 1)),
                  pl.BlockSpec((seq, KV_W), lambda b, i: (b, 2)),
                  pl.BlockSpec((seq, KV_W), lambda b, i: (b, 2)),
                  pl.BlockSpec((seq, LANES), lambda b, i: (0, 0)),
                  pl.BlockSpec((1,) + wbias.shape[1:], lambda b, i: (jnp.minimum(i, n_var - 1), 0, 0)),
                  pl.BlockSpec((1, NSA_KV_HEADS, Q, LANES), lambda b, i: (b, 0, i, 0)),
                  pl.BlockSpec((Q, LANES), qmap),
                  pl.BlockSpec((Q, NSA_W), qmap)],
        out_specs=pl.BlockSpec((Q, NSA_W), qmap),
        out_shape=jax.ShapeDtypeStruct((M, NSA_W), BF16),
        scratch_shapes=[pltpu.VMEM((NSA_KV_HEADS * G * Q, LANES), F32),
                        pltpu.VMEM((NSA_KV_HEADS * G * Q, 2 * LANES), F32)],
        compiler_params=_params(2),
        name="nsa_slc_win",
    )(q, k, v, k, v, onehot, wbias, sel, gates, o_cmp)


def _outproj_kernel(oh_ref, on_ref, og_ref, w_ref, x_ref, g_ref, lw_ref, lb_ref, o_ref, *, alpha):
    mix = (_dot(oh_ref[...], w_ref[0:HG_W, :])
           + _dot(on_ref[...], w_ref[HG_W:HG_W + NSA_W, :])
           + _dot(og_ref[...], w_ref[HG_W + NSA_W:HG_W + NSA_W + GM_W, :]))
    y = alpha * x_ref[...] + (1.0 + g_ref[0]) * mix
    o_ref[...] = _layer_norm(y, lw_ref[...], lb_ref[...])


def _out_projection(o_h, o_n, o_g, w_o, x2, g1, ln_w, ln_b, seq, alpha):
    M, D = x2.shape
    tm = ROW_TILE
    per_b = seq // tm
    row = lambda i: (i, 0)
    const = lambda i: (0, 0)
    return pl.pallas_call(
        functools.partial(_outproj_kernel, alpha=alpha),
        grid=(M // tm,),
        in_specs=[pl.BlockSpec((tm, HG_W), row),
                  pl.BlockSpec((tm, NSA_W), row),
                  pl.BlockSpec((tm, GM_W), row),
                  pl.BlockSpec(w_o.shape, const),
                  pl.BlockSpec((tm, D), row),
                  pl.BlockSpec((1, 1, D), lambda i: (i // per_b, 0, 0)),
                  pl.BlockSpec((1, D), const),
                  pl.BlockSpec((1, D), const)],
        out_specs=pl.BlockSpec((tm, D), row),
        out_shape=jax.ShapeDtypeStruct((M, D), F32),
        compiler_params=_params(1),
        name="out_proj_ln",
    )(o_h, o_n, o_g, w_o, x2, g1, ln_w, ln_b)


def _ffn_kernel(x_ref, sc_ref, sh_ref, g_ref, w1_ref, w2_ref, lw_ref, lb_ref, o_ref, *, alpha, ff_chunk):
    x = x_ref[...]
    h = (x * (1.0 + sc_ref[0]) + sh_ref[0]).astype(BF16)
    y = jnp.zeros(x.shape, F32)
    for c in range(w1_ref.shape[1] // ff_chunk):
        a = jnp.maximum(_dot(h, w1_ref[:, c * ff_chunk:(c + 1) * ff_chunk]), 0.0)
        y = y + _dot((a * a).astype(BF16), w2_ref[c * ff_chunk:(c + 1) * ff_chunk, :])
    z = alpha * x + (1.0 + g_ref[0]) * y
    o_ref[...] = _layer_norm(z, lw_ref[...], lb_ref[...])


def _ffn(x2, sc, sh, g2, w1, w2, ln_w, ln_b, seq, alpha):
    M, D = x2.shape
    tm = ROW_TILE
    per_b = seq // tm
    row = lambda i: (i, 0)
    const = lambda i: (0, 0)
    bmap = lambda i: (i // per_b, 0, 0)
    return pl.pallas_call(
        functools.partial(_ffn_kernel, alpha=alpha, ff_chunk=1024),
        grid=(M // tm,),
        in_specs=[pl.BlockSpec((tm, D), row),
                  pl.BlockSpec((1, 1, D), bmap),
                  pl.BlockSpec((1, 1, D), bmap),
                  pl.BlockSpec((1, 1, D), bmap),
                  pl.BlockSpec(w1.shape, const),
                  pl.BlockSpec(w2.shape, const),
                  pl.BlockSpec((1, D), const),
                  pl.BlockSpec((1, D), const)],
        out_specs=pl.BlockSpec((tm, D), row),
        out_shape=jax.ShapeDtypeStruct((M, D), F32),
        compiler_params=_params(1),
        name="ffn_ln",
    )(x2, sc, sh, g2, w1, w2, ln_w, ln_b)


def _segments(a, batch, seq):
    a = a.reshape(batch, seq // CMP_STRIDE, CMP_STRIDE, NSA_KV_HEADS, HEAD_DIM)
    return a.transpose(0, 3, 1, 2, 4).reshape(batch * NSA_KV_HEADS, seq // CMP_STRIDE, CMP_STRIDE * HEAD_DIM)


def kernel(x, c, positions, w_in, w_o, hgrn_lower_bounds, hgrn_norm_w, cmp_pe_k, cmp_w1_k, cmp_w2_k, cmp_pe_v, cmp_w1_v, cmp_w2_v, gmlp_norm_w, gmlp_norm_b, gmlp_w_s, gmlp_b_s, w_ff1, w_ff2, w_ada, b_ada, ln1_w, ln1_b, ln2_w, ln2_b):
    B, S, D = x.shape
    L = w_in.shape[0]
    M = B * S
    alpha = (2 * L) ** 0.25
    n_seg = S // CMP_STRIDE

    inv_freq = ROPE_THETA ** (-jnp.arange(0, HEAD_DIM, 2, dtype=F32) / HEAD_DIM)
    ang = positions.astype(F32)[..., None] * inv_freq
    cos, sin = jnp.cos(ang).reshape(M, HEAD_DIM // 2), jnp.sin(ang).reshape(M, HEAD_DIM // 2)
    cos_f = jnp.concatenate([cos, cos, cos, cos], axis=-1)
    sin_s = jnp.concatenate([-sin, sin, -sin, sin], axis=-1)

    lb_sm = jax.nn.softmax(hgrn_lower_bounds.astype(F32), axis=0)
    lb_all = jnp.cumsum(lb_sm, axis=0) - lb_sm[0:1]

    col_idx, n_cols = _in_col_index()
    w_cat = jnp.where(jnp.asarray(col_idx >= 0)[None, None, :],
                      jnp.take(w_in, jnp.asarray(np.maximum(col_idx, 0)), axis=2), 0.0).astype(BF16)
    assert n_cols == w_in.shape[2]

    lane_head = np.arange(HG_W) // HEAD_DIM
    bd = jnp.asarray((lane_head[:, None] == lane_head[None, :]).astype(np.float32), dtype=BF16)
    hg_window, hg_pair_mask = _hgrn_tables()
    hg_window = jnp.asarray(hg_window, dtype=BF16)
    hg_pair_mask = jnp.asarray(hg_pair_mask, dtype=F32)
    cs = np.arange(n_seg) * CMP_STRIDE
    ss = np.arange(LANES) * SLC_BLOCK
    ov = (cs[:, None] < ss[None, :] + SLC_BLOCK) & (ss[None, :] < cs[:, None] + CMP_LEN)
    ov &= (np.arange(n_seg) < n_seg - 1)[:, None] & (np.arange(LANES) < S // SLC_BLOCK)[None, :]
    overlap = jnp.asarray(ov.T.astype(np.float32), dtype=BF16)
    assert S % SLC_KEYS == 0 and S // SLC_BLOCK <= LANES
    onehot = (jnp.arange(S)[:, None] // SLC_BLOCK == jnp.arange(LANES)[None, :]).astype(BF16)
    wbias = jnp.asarray(_window_bias())
    o_perm = HG_W + _nsa_head_perm()
    w_o = jnp.concatenate([w_o[:, :HG_W], w_o[:, o_perm], w_o[:, HG_W + NSA_W:]], axis=1)

    c8 = jnp.pad(c, ((0, 8 - B), (0, 0)))
    mod = _modulation(c8, w_ada, b_ada)[:, :B]

    x2 = x.reshape(M, D)
    for l in range(L):
        sh1, sc1, g1, sh2, sc2, g2 = [mod[l, :, i * D:(i + 1) * D].reshape(B, 1, D) for i in range(6)]
        hg, q, k, v, gm, gates = _in_projection(x2, sc1, sh1, w_cat[l], cos_f, sin_s, S)

        o_h = _hgrn2(hg, lb_all[l].reshape(1, HG_W), jnp.tile(hgrn_norm_w[l], HG_HEADS).reshape(1, HG_W), bd,
                     hg_window, hg_pair_mask, S)

        bias = jnp.repeat(gmlp_b_s[l].T, HEAD_DIM, axis=1)
        o_g = _gmlp(gm, gmlp_norm_w[l].reshape(1, GM_W), gmlp_norm_b[l].reshape(1, GM_W), gmlp_w_s[l], bias)

        pe_k = jnp.pad(cmp_pe_k[l].reshape(1, -1), ((0, 7), (0, 0)))
        pe_v = jnp.pad(cmp_pe_v[l].reshape(1, -1), ((0, 7), (0, 0)))
        kc = _compress(_segments(k[:, 0:KV_W], B, S), pe_k, cmp_w1_k[l].astype(BF16), cmp_w2_k[l].astype(BF16))
        vc = _compress(_segments(v[:, 0:KV_W], B, S), pe_v, cmp_w1_v[l].astype(BF16), cmp_w2_v[l].astype(BF16))
        kc = kc.reshape(B, NSA_KV_HEADS, n_seg, HEAD_DIM)
        vc = vc.reshape(B, NSA_KV_HEADS, n_seg, HEAD_DIM)
        o_cmp, sel = _cmp_select(q, kc, vc, overlap, B, S)
        o_n = _slc_win(q, k, v, onehot, wbias, sel, gates, o_cmp, B, S)

        x2 = _out_projection(o_h, o_n, o_g, w_o[l].astype(BF16), x2, g1,
                             ln1_w[l].reshape(1, D), ln1_b[l].reshape(1, D), S, alpha)
        x2 = _ffn(x2, sc2, sh2, g2, w_ff1[l].astype(BF16), w_ff2[l].astype(BF16),
                  ln2_w[l].reshape(1, D), ln2_b[l].reshape(1, D), S, alpha)
    return x2.reshape(B, S, D)
```

```python
import functools

import numpy as np
import jax
import jax.numpy as jnp
from jax import lax
from jax.experimental import pallas as pl
from jax.experimental.pallas import tpu as pltpu

F32 = jnp.float32
BF16 = jnp.bfloat16

HEAD_DIM = 64
HG_HEADS = 4
HG_W = HG_HEADS * HEAD_DIM
NSA_HEADS = 8
NSA_W = NSA_HEADS * HEAD_DIM
NSA_GQA = 4
NSA_KV_HEADS = NSA_HEADS // NSA_GQA
KV_W = NSA_KV_HEADS * HEAD_DIM
GM_HEADS = 4
GM_W = GM_HEADS * HEAD_DIM
N_GATES = 3 * NSA_HEADS
ROPE_THETA = 10000.0
HGRN_CHUNK = 64
CMP_LEN = 32
CMP_STRIDE = 16
SLC_BLOCK = 64
SLC_TOPN = 16
WINDOW = 512
Q_BLOCK = 128
GMLP_CHUNK = 128
NEG = -1e30
BIG = 1e30
F_MIN = 1e-30
LANES = 128
ROW_TILE = 512
SLC_KEYS = 1024
SLC_ROW_BLOCK = 256
CMP_ROW_BLOCK = 512
Q_SCALE = HEAD_DIM ** -0.5 * 1.4426950408889634
VMEM_LIMIT = 56 * 1024 * 1024


def _params(n_axes, semantics="parallel"):
    return pltpu.CompilerParams(dimension_semantics=(semantics,) * n_axes, vmem_limit_bytes=VMEM_LIMIT)


def _dot(a, b):
    return jnp.dot(a, b, preferred_element_type=F32)


def _dot_nt(a, b):
    return lax.dot_general(a, b, (((1,), (1,)), ((), ())), preferred_element_type=F32)


def _split2(x):
    hi = x.astype(BF16)
    lo = (x - hi.astype(F32)).astype(BF16)
    return hi, lo


def _split3(x):
    hi = x.astype(BF16)
    r = x - hi.astype(F32)
    mid = r.astype(BF16)
    lo = (r - mid.astype(F32)).astype(BF16)
    return hi, mid, lo


def _tile_lanes(x, n):
    return x if n == 1 else jnp.concatenate([x] * n, axis=1)


def _row_max(s):
    return jnp.broadcast_to(jnp.max(s, axis=-1, keepdims=True), (s.shape[0], LANES))


def _row_sum(s):
    return jnp.broadcast_to(jnp.sum(s, axis=-1, keepdims=True), (s.shape[0], LANES))


def _layer_norm(y, w, b):
    mu = jnp.mean(y, axis=-1, keepdims=True)
    d = y - mu
    var = jnp.mean(d * d, axis=-1, keepdims=True)
    return d * lax.rsqrt(var + 1e-5) * w + b


def _mod_kernel(c_ref, w_ref, b_ref, o_ref):
    a = c_ref[...]
    a = a * jax.nn.sigmoid(a)
    ah, al = _split2(a)
    wh, wl = _split2(w_ref[0])
    o_ref[0] = _dot(ah, wh) + _dot(al, wh) + _dot(ah, wl) + b_ref[0]


def _modulation(c8, w_ada, b_ada):
    L, D, N = w_ada.shape
    tn = 1024
    return pl.pallas_call(
        _mod_kernel,
        grid=(L, N // tn),
        in_specs=[pl.BlockSpec((8, D), lambda l, j: (0, 0)),
                  pl.BlockSpec((1, D, tn), lambda l, j: (l, 0, j)),
                  pl.BlockSpec((1, 1, tn), lambda l, j: (l, 0, j))],
        out_specs=pl.BlockSpec((1, 8, tn), lambda l, j: (l, 0, j)),
        out_shape=jax.ShapeDtypeStruct((L, 8, N), F32),
        compiler_params=_params(2),
        name="adaln_mod",
    )(c8, w_ada, b_ada.reshape(L, 1, N))


C_HG = 0
C_Q = C_HG + 4 * HG_W
C_K = C_Q + NSA_W
C_V = C_K + 3 * KV_W
C_GM = C_V + 3 * KV_W
C_GT = C_GM + 2 * GM_W
C_END = C_GT + LANES


def _nsa_head_perm():
    heads = [hk * NSA_GQA + g for g in range(NSA_GQA) for hk in range(NSA_KV_HEADS)]
    return np.concatenate([np.arange(h * HEAD_DIM, (h + 1) * HEAD_DIM) for h in heads])


def _in_col_index():
    o = {}
    off = 0
    for name, size in (("hq", HG_W), ("hf", HG_W), ("hi", HG_W), ("hg", HG_W), ("nq", NSA_W),
                       ("kcm", KV_W), ("vcm", KV_W), ("ksl", KV_W), ("vsl", KV_W), ("kwn", KV_W),
                       ("vwn", KV_W), ("ngt", N_GATES), ("gu", GM_W), ("gv", GM_W)):
        o[name] = np.arange(off, off + size)
        off += size
    idx = np.concatenate([o["hq"], o["hf"], o["hi"], o["hg"], o["nq"][_nsa_head_perm()],
                          o["kcm"], o["ksl"], o["kwn"], o["vcm"], o["vsl"], o["vwn"],
                          o["gu"], o["gv"], o["ngt"], np.full(LANES - N_GATES, -1)])
    return idx, off


def _rope(a, cos_f, sin_s):
    lane = lax.broadcasted_iota(jnp.int32, a.shape, 1)
    first = (lane & (HEAD_DIM - 1)) < HEAD_DIM // 2
    partner = jnp.where(first, pltpu.roll(a, LANES - HEAD_DIM // 2, 1), pltpu.roll(a, HEAD_DIM // 2, 1))
    return a * cos_f + partner * sin_s


def _inproj_kernel(x_ref, sc_ref, sh_ref, w_ref, cos_ref, sin_ref,
                   hg_ref, q_ref, k_ref, v_ref, kc_ref, vc_ref, gm_ref, gt_ref):
    h = (x_ref[...] * (1.0 + sc_ref[0]) + sh_ref[0]).astype(BF16)
    cos_f = cos_ref[...]
    sin_s = sin_ref[...]
    hg_ref[...] = _dot(h, w_ref[:, C_HG:C_Q])
    for j in range(NSA_W // LANES):
        a = _dot(h, w_ref[:, C_Q + j * LANES:C_Q + (j + 1) * LANES])
        q_ref[:, j * LANES:(j + 1) * LANES] = (_rope(a, cos_f, sin_s) * Q_SCALE).astype(BF16)
    kcm = _rope(_dot(h, w_ref[:, C_K:C_K + LANES]), cos_f, sin_s).astype(BF16)
    vcm = _dot(h, w_ref[:, C_V:C_V + LANES]).astype(BF16)
    for hk in range(NSA_KV_HEADS):
        kc_ref[hk] = kcm[:, hk * HEAD_DIM:(hk + 1) * HEAD_DIM]
        vc_ref[hk] = vcm[:, hk * HEAD_DIM:(hk + 1) * HEAD_DIM]
    for j in range(2):
        a = _dot(h, w_ref[:, C_K + (j + 1) * LANES:C_K + (j + 2) * LANES])
        k_ref[:, j * LANES:(j + 1) * LANES] = _rope(a, cos_f, sin_s).astype(BF16)
    v_ref[...] = _dot(h, w_ref[:, C_V + LANES:C_GM]).astype(BF16)
    gm_ref[...] = _dot(h, w_ref[:, C_GM:C_GT])
    gt_ref[...] = jax.nn.sigmoid(_dot(h, w_ref[:, C_GT:C_END]))


def _in_projection(x2, sc, sh, w_cat, cos_f, sin_s, seq):
    M, D = x2.shape
    tm = ROW_TILE
    per_b = seq // tm
    row = lambda i: (i, 0)
    bmap = lambda i: (i // per_b, 0, 0)
    return pl.pallas_call(
        _inproj_kernel,
        grid=(M // tm,),
        in_specs=[pl.BlockSpec((tm, D), row),
                  pl.BlockSpec((1, 1, D), bmap),
                  pl.BlockSpec((1, 1, D), bmap),
                  pl.BlockSpec((D, C_END), lambda i: (0, 0)),
                  pl.BlockSpec((tm, LANES), row),
                  pl.BlockSpec((tm, LANES), row)],
        out_specs=[pl.BlockSpec((tm, 4 * HG_W), row),
                   pl.BlockSpec((tm, NSA_W), row),
                   pl.BlockSpec((tm, 2 * KV_W), row),
                   pl.BlockSpec((tm, 2 * KV_W), row),
                   pl.BlockSpec((NSA_KV_HEADS, tm, HEAD_DIM), lambda i: (0, i, 0)),
                   pl.BlockSpec((NSA_KV_HEADS, tm, HEAD_DIM), lambda i: (0, i, 0)),
                   pl.BlockSpec((tm, 2 * GM_W), row),
                   pl.BlockSpec((tm, LANES), row)],
        out_shape=[jax.ShapeDtypeStruct((M, 4 * HG_W), F32),
                   jax.ShapeDtypeStruct((M, NSA_W), BF16),
                   jax.ShapeDtypeStruct((M, 2 * KV_W), BF16),
                   jax.ShapeDtypeStruct((M, 2 * KV_W), BF16),
                   jax.ShapeDtypeStruct((NSA_KV_HEADS, M, HEAD_DIM), BF16),
                   jax.ShapeDtypeStruct((NSA_KV_HEADS, M, HEAD_DIM), BF16),
                   jax.ShapeDtypeStruct((M, 2 * GM_W), F32),
                   jax.ShapeDtypeStruct((M, LANES), F32)],
        compiler_params=_params(1),
        name="in_proj",
    )(x2, sc, sh, w_cat, cos_f, sin_s)


HGRN_LEVELS = 6
HGRN_GROUP = 2


def _hgrn_tables():
    C = HGRN_CHUNK
    r = np.arange(C)
    wins = [r[None, :] <= r[:, None], r[None, :] > r[:, None]]
    masks = []
    for lvl in range(HGRN_LEVELS):
        c = C >> (lvl + 1)
        blk = r // c
        pair = blk // 2
        upper = blk % 2 == 1
        bound = (2 * pair + 1) * c - 1
        win_u = (r[None, :] > bound[:, None]) & (r[None, :] <= r[:, None])
        win_l = (r[None, :] > r[:, None]) & (r[None, :] <= bound[:, None])
        wins.append(np.where(upper[:, None], win_u, win_l))
        masks.append(upper[:, None] & ~upper[None, :] & (pair[:, None] == pair[None, :]))
    masks.append(np.eye(C, dtype=bool))
    window = np.concatenate(wins, axis=0).astype(np.float32)
    pair_mask = np.stack([np.tile(m, (1, HG_HEADS)) for m in masks]).astype(np.float32)
    return window, pair_mask


def _hgrn_kernel(x_ref, lb_ref, nw_ref, bd_ref, win_ref, pm_ref, o_ref, st_ref, *, steps_per_seq):
    C = HGRN_CHUNK
    W = HG_W

    @pl.when(pl.program_id(0) % steps_per_seq == 0)
    def _():
        st_ref[...] = jnp.zeros_like(st_ref)

    lb = lb_ref[...]
    nw = nw_ref[...]
    bd = bd_ref[...]
    bd_f = bd.astype(F32)
    lane_head = lax.broadcasted_iota(jnp.int32, (C, W), 1) >> 6
    own = [lane_head == h for h in range(HG_HEADS)]

    def stack(a):
        return jnp.concatenate([jnp.where(own[h], a, 0.0) for h in range(HG_HEADS)], axis=0).astype(BF16)

    def intra(r0):
        q = x_ref[pl.ds(r0, C), 0:W]
        z = x_ref[pl.ds(r0, C), W:2 * W]
        v = x_ref[pl.ds(r0, C), 2 * W:3 * W]
        f = lb + (1.0 - lb) * jax.nn.sigmoid(z)
        log_f = jnp.log(jnp.maximum(f, F_MIN))
        kk = (1.0 - lb) * jax.nn.sigmoid(-z)
        qf = q * jax.nn.sigmoid(q)
        l_hi, l_lo = _split2(log_f)
        xw = _dot(win_ref[...], jnp.concatenate([l_hi, l_lo], axis=1))
        e = jnp.exp(xw[:, 0:W] + xw[:, W:2 * W])
        e_b = e[0:C]
        attn = _dot_nt(qf.astype(BF16), stack(kk)) * pm_ref[HGRN_LEVELS]
        for lvl in range(HGRN_LEVELS):
            e_l = e[(2 + lvl) * C:(3 + lvl) * C]
            attn = attn + _dot_nt((qf * e_l).astype(BF16), stack(kk * e_l)) * pm_ref[lvl]
        o_intra = _dot(attn.astype(BF16), stack(v))
        upd = lax.dot_general(v.astype(BF16), (kk * e[C:2 * C]).astype(BF16), (((0,), (0,)), ((), ())),
                              preferred_element_type=F32) * bd_f
        return o_intra, (qf * e_b).astype(BF16), upd, e_b[C - 1:C, :]

    def finish(r0, o_intra, q_decayed, upd, e_last):
        st = st_ref[...]
        o = o_intra + _dot_nt(q_decayed, st.astype(BF16))
        st_ref[...] = st * e_last + upd
        oo = o * o
        o_hi, o_lo = _split2(oo)
        ms = (_dot(o_hi, bd) + _dot(o_lo, bd)) * (1.0 / HEAD_DIM)
        y = o * lax.rsqrt(ms + 1e-6) * nw
        g = x_ref[pl.ds(r0, C), 3 * W:4 * W]
        o_ref[pl.ds(r0, C), :] = (y * (g * jax.nn.sigmoid(g))).astype(o_ref.dtype)

    def chunk_group(ci, carry):
        rows = [pl.multiple_of(ci * (HGRN_GROUP * C) + i * C, C) for i in range(HGRN_GROUP)]
        parts = [intra(r) for r in rows]
        for r, part in zip(rows, parts):
            finish(r, *part)
        return carry

    lax.fori_loop(0, x_ref.shape[0] // (HGRN_GROUP * C), chunk_group, 0)


def _hgrn2(hg, lb, nw, bd, window, pair_mask, seq):
    M = hg.shape[0]
    tm = ROW_TILE
    return pl.pallas_call(
        functools.partial(_hgrn_kernel, steps_per_seq=seq // tm),
        grid=(M // tm,),
        in_specs=[pl.BlockSpec((tm, 4 * HG_W), lambda i: (i, 0)),
                  pl.BlockSpec((1, HG_W), lambda i: (0, 0)),
                  pl.BlockSpec((1, HG_W), lambda i: (0, 0)),
                  pl.BlockSpec((HG_W, HG_W), lambda i: (0, 0)),
                  pl.BlockSpec(window.shape, lambda i: (0, 0)),
                  pl.BlockSpec(pair_mask.shape, lambda i: (0, 0, 0))],
        out_specs=pl.BlockSpec((tm, HG_W), lambda i: (i, 0)),
        out_shape=jax.ShapeDtypeStruct((M, HG_W), BF16),
        scratch_shapes=[pltpu.VMEM((HG_W, HG_W), F32)],
        compiler_params=_params(1, "arbitrary"),
        name="hgrn2",
    )(hg, lb, nw, bd, window, pair_mask)


def _gelu(x):
    return jax.nn.gelu(x)


def _gmlp_kernel(x_ref, nw_ref, nb_ref, ws_ref, bias_ref, o_ref):
    T = GMLP_CHUNK
    u = _gelu(x_ref[:, 0:GM_W])
    v = _layer_norm(_gelu(x_ref[:, GM_W:2 * GM_W]), nw_ref[...], nb_ref[...]).astype(BF16)
    lower = lax.broadcasted_iota(jnp.int32, (T, T), 0) >= lax.broadcasted_iota(jnp.int32, (T, T), 1)
    head = lax.broadcasted_iota(jnp.int32, (T, GM_W), 1) >> 6
    bias = bias_ref[...]
    for c in range(x_ref.shape[0] // T):
        vc = v[c * T:(c + 1) * T]
        sv = bias
        for g in range(GM_HEADS):
            w = jnp.where(lower, ws_ref[g], 0.0).astype(BF16)
            sv = sv + jnp.where(head == g, _dot(w, vc), 0.0)
        o_ref[c * T:(c + 1) * T, :] = (u[c * T:(c + 1) * T] * sv).astype(o_ref.dtype)


def _gmlp(gm, nw, nb, w_s, bias):
    M = gm.shape[0]
    tm = ROW_TILE
    T = GMLP_CHUNK
    return pl.pallas_call(
        _gmlp_kernel,
        grid=(M // tm,),
        in_specs=[pl.BlockSpec((tm, 2 * GM_W), lambda i: (i, 0)),
                  pl.BlockSpec((1, GM_W), lambda i: (0, 0)),
                  pl.BlockSpec((1, GM_W), lambda i: (0, 0)),
                  pl.BlockSpec((GM_HEADS, T, T), lambda i: (0, 0, 0)),
                  pl.BlockSpec((T, GM_W), lambda i: (0, 0))],
        out_specs=pl.BlockSpec((tm, GM_W), lambda i: (i, 0)),
        out_shape=jax.ShapeDtypeStruct((M, GM_W), BF16),
        compiler_params=_params(1),
        name="gmlp",
    )(gm, nw, nb, w_s, bias)


def _compress_kernel(a_ref, pe_ref, w1_ref, w2_ref, o_ref):
    half = w1_ref.shape[0] // 2
    n_seg = a_ref.shape[2]
    a = jnp.concatenate([a_ref[hk, 0] for hk in range(NSA_KV_HEADS)], axis=0)
    u = _dot(a, w1_ref[0:half, :])
    v = _dot(a, w1_ref[half:2 * half, :])
    pe_hi, pe_lo = _split2(pe_ref[...])
    c0 = (_dot(pe_hi, w1_ref[...]) + _dot(pe_lo, w1_ref[...]))[0:1, :]
    h1 = u + pltpu.roll(v, NSA_KV_HEADS * n_seg - 1, 0) + c0
    out = _dot(_gelu(h1).astype(BF16), w2_ref[...])
    valid = (lax.broadcasted_iota(jnp.int32, out.shape, 0) & (n_seg - 1)) < n_seg - 1
    out = jnp.where(valid, out, 0.0).astype(o_ref.dtype)
    for hk in range(NSA_KV_HEADS):
        o_ref[0, :, hk * HEAD_DIM:(hk + 1) * HEAD_DIM] = out[hk * n_seg:(hk + 1) * n_seg]


def _compress(a, pe8, w1, w2):
    _, batch, n_seg, feat = a.shape
    assert n_seg & (n_seg - 1) == 0
    return pl.pallas_call(
        _compress_kernel,
        grid=(batch,),
        in_specs=[pl.BlockSpec((NSA_KV_HEADS, 1, n_seg, feat), lambda i: (0, i, 0, 0)),
                  pl.BlockSpec(pe8.shape, lambda i: (0, 0)),
                  pl.BlockSpec(w1.shape, lambda i: (0, 0)),
                  pl.BlockSpec(w2.shape, lambda i: (0, 0))],
        out_specs=pl.BlockSpec((1, n_seg, KV_W), lambda i: (i, 0, 0)),
        out_shape=jax.ShapeDtypeStruct((batch, n_seg, KV_W), BF16),
        compiler_params=_params(1),
        name="nsa_compress",
    )(a, pe8, w1, w2)


def _stacked_queries(q_ref):
    half = lax.broadcasted_iota(jnp.int32, (Q_BLOCK, LANES), 1) >> 6
    zero = jnp.zeros((), q_ref.dtype)
    return jnp.concatenate([jnp.where(half == hk, q_ref[:, g * LANES:(g + 1) * LANES], zero)
                            for hk in range(NSA_KV_HEADS) for g in range(NSA_GQA)], axis=0)


def _cmp_select_kernel(q_ref, kc_ref, vc_ref, ov_ref, o_ref, sel_ref, o_s, *, n_blocks):
    Q = Q_BLOCK
    G = NSA_GQA
    HK = NSA_KV_HEADS
    RB = CMP_ROW_BLOCK
    n_cmp = kc_ref.shape[1]
    n_rep = n_cmp // LANES
    q0 = pl.program_id(1) * Q
    qpad = _stacked_queries(q_ref)
    kc = kc_ref[0]
    vc = vc_ref[0]
    t_row = q0 + (lax.broadcasted_iota(jnp.int32, (RB, 1), 0) & (Q - 1))
    valid = lax.broadcasted_iota(jnp.int32, (RB, n_cmp), 1) * CMP_STRIDE + (CMP_LEN - 1) <= t_row
    any_valid = jnp.broadcast_to(t_row >= CMP_LEN - 1, (RB, LANES))
    p_sum = [None] * HK
    for rb in range(HK * G * Q // RB):
        s = _dot_nt(qpad[rb * RB:(rb + 1) * RB], kc)
        sm = jnp.where(valid, s, NEG)
        e = jnp.exp2(sm - _tile_lanes(_row_max(sm), n_rep))
        p = e * _tile_lanes(jnp.where(any_valid, 1.0 / _row_sum(e), 0.0), n_rep)
        o_s[rb * RB:(rb + 1) * RB, :] = _dot(p.astype(BF16), vc)
        for i in range(RB // Q):
            hk = (rb * RB + i * Q) // (G * Q)
            blk = p[i * Q:(i + 1) * Q]
            p_sum[hk] = blk if p_sum[hk] is None else p_sum[hk] + blk
    half = lax.broadcasted_iota(jnp.int32, (Q, LANES), 1) >> 6
    for g in range(G):
        o_ref[:, g * LANES:(g + 1) * LANES] = jnp.where(half == 0, o_s[g * Q:(g + 1) * Q, :],
                                                        o_s[(G + g) * Q:(G + g + 1) * Q, :])

    ov_t = ov_ref[...]
    imp = []
    for hk in range(HK):
        p_hi, p_lo = _split2(p_sum[hk])
        imp.append(_dot_nt(ov_t, p_hi) + _dot_nt(ov_t, p_lo))
    imp = jnp.concatenate(imp, axis=1)
    j = lax.broadcasted_iota(jnp.int32, (LANES, HK * Q), 0)
    j_f = j.astype(F32)
    tb = (q0 + (lax.broadcasted_iota(jnp.int32, (LANES, HK * Q), 1) & (Q - 1))) >> 6
    forced = (j == 0) | (j == tb) | (j == tb - 1)
    score = jnp.where(j > tb, NEG, jnp.where(forced, BIG, imp))
    score = jnp.where(j < n_blocks, score, -jnp.inf)
    sel = jnp.zeros((LANES, HK * Q), F32)
    for _ in range(min(SLC_TOPN, n_blocks)):
        top = jnp.max(score, axis=0, keepdims=True)
        first = jnp.min(jnp.where(score == top, j_f, float(LANES)), axis=0, keepdims=True)
        hit = j_f == first
        sel = jnp.where(hit, 1.0, sel)
        score = jnp.where(hit, -jnp.inf, score)
    neg_sel = jnp.where(sel > 0.5, 0.0, NEG)
    for hk in range(HK):
        sel_ref[0, hk] = neg_sel[:, hk * Q:(hk + 1) * Q].T.astype(sel_ref.dtype)


def _cmp_select(q, kc, vc, overlap, batch, seq):
    M = q.shape[0]
    Q = Q_BLOCK
    nq = seq // Q
    n_cmp = kc.shape[1]
    return pl.pallas_call(
        functools.partial(_cmp_select_kernel, n_blocks=seq // SLC_BLOCK),
        grid=(batch, nq),
        in_specs=[pl.BlockSpec((Q, NSA_W), lambda b, i: (b * nq + i, 0)),
                  pl.BlockSpec((1, n_cmp, KV_W), lambda b, i: (b, 0, 0)),
                  pl.BlockSpec((1, n_cmp, KV_W), lambda b, i: (b, 0, 0)),
                  pl.BlockSpec((LANES, n_cmp), lambda b, i: (0, 0))],
        out_specs=[pl.BlockSpec((Q, NSA_W), lambda b, i: (b * nq + i, 0)),
                   pl.BlockSpec((1, NSA_KV_HEADS, Q, LANES), lambda b, i: (b, 0, i, 0))],
        out_shape=[jax.ShapeDtypeStruct((M, NSA_W), F32),
                   jax.ShapeDtypeStruct((batch, NSA_KV_HEADS, seq, LANES), BF16)],
        scratch_shapes=[pltpu.VMEM((NSA_HEADS * Q, LANES), F32)],
        compiler_params=_params(2),
        name="nsa_cmp_select",
    )(q, kc, vc, overlap)


def _window_bias():
    r = np.arange(Q_BLOCK)[:, None]
    j = np.arange(WINDOW + Q_BLOCK)[None, :]
    band = (j > r) & (j <= r + WINDOW)
    tabs = [np.where(band & (j >= WINDOW - Q_BLOCK * v), 0.0, NEG) for v in range(WINDOW // Q_BLOCK + 1)]
    return np.stack(tabs).astype(np.float32)


def _slc_win_kernel(q_ref, ks_ref, vs_ref, kw_ref, vw_ref, oh_ref, wb_ref, sel_ref, gt_ref, oc_ref, o_ref,
                    m_s, acc_s):
    Q = Q_BLOCK
    G = NSA_GQA
    HK = NSA_KV_HEADS
    R = HK * G * Q
    RB = SLC_ROW_BLOCK
    KC = SLC_KEYS
    NW = WINDOW // Q + 1
    q0 = pl.program_id(1) * Q
    n_full = q0 // KC
    t_row = q0 + (lax.broadcasted_iota(jnp.int32, (RB, 1), 0) & (Q - 1))
    key_off = lax.broadcasted_iota(jnp.int32, (RB, KC), 1)
    half_q = lax.broadcasted_iota(jnp.int32, (Q, LANES), 1) >> 6
    qpad = _stacked_queries(q_ref)
    qaug =jnp.concatenate([qpad, jnp.concatenate([sel_ref[0, hk] for hk in range(HK) for _ in range(G)], axis=0)],
                           axis=1)
    m_s[...] = jnp.full(m_s.shape, NEG, F32)
    acc_s[...] = jnp.zeros(acc_s.shape, F32)

    def slc_chunk(c, causal):
        k0 = pl.multiple_of(c * KC, KC)
        kaug = jnp.concatenate([ks_ref[pl.ds(k0, KC), :], oh_ref[pl.ds(k0, KC), :]], axis=1)
        vaug = jnp.concatenate([vs_ref[pl.ds(k0, KC), :], jnp.ones((KC, LANES), BF16)], axis=1)
        for rb in range(R // RB):
            rows = slice(rb * RB, (rb + 1) * RB)
            s = _dot_nt(qaug[rows], kaug)
            if causal:
                s = jnp.where(k0 + key_off <= t_row, s, NEG)
            m_old = m_s[rows, :]
            m_new = jnp.maximum(m_old, jnp.max(s, axis=-1, keepdims=True))
            p = jnp.exp2(s - _tile_lanes(m_new, KC // LANES)).astype(BF16)
            acc_s[rows, :] = _tile_lanes(jnp.exp2(m_old - m_new), 2) * acc_s[rows, :] + _dot(p, vaug)
            m_s[rows, :] = m_new

    def body(c, carry):
        slc_chunk(c, False)
        return carry

    lax.fori_loop(0, n_full, body, 0)
    slc_chunk(n_full, True)
    acc = acc_s[...]
    o_slc = acc[:, 0:LANES] / acc[:, LANES:2 * LANES]

    rows_k, rows_v = [], []
    for w in range(NW):
        k0 = pl.multiple_of(jnp.maximum(q0 - WINDOW + w * Q, 0), Q)
        rows_k.append(kw_ref[pl.ds(k0, Q), :])
        rows_v.append(vw_ref[pl.ds(k0, Q), :])
    k_win = jnp.concatenate(rows_k, axis=0)
    v_win = jnp.concatenate([jnp.concatenate(rows_v, axis=0), jnp.ones((NW * Q, LANES), BF16)], axis=1)
    s = _dot_nt(qpad, k_win) + jnp.concatenate([wb_ref[0]] * (HK * G), axis=0)
    p = jnp.exp2(s - _tile_lanes(_row_max(s), NW)).astype(BF16)
    acc = _dot(p, v_win)
    o_win = acc[:, 0:LANES] / acc[:, LANES:2 * LANES]

    gates = gt_ref[...]
    for g in range(G):
        oc = oc_ref[:, g * LANES:(g + 1) * LANES]
        tiles = []
        for hk in range(HK):
            c0 = 3 * (hk * G + g)
            rows = slice((hk * G + g) * Q, (hk * G + g + 1) * Q)
            tiles.append(gates[:, c0:c0 + 1] * oc + gates[:, c0 + 1:c0 + 2] * o_slc[rows]
                         + gates[:, c0 + 2:c0 + 3] * o_win[rows])
        o_ref[:, g * LANES:(g + 1) * LANES] = jnp.where(half_q == 0, tiles[0], tiles[1]).astype(o_ref.dtype)


def _slc_win(q, k, v, onehot, wbias, sel, gates, o_cmp, batch, seq):
    M = q.shape[0]
    Q = Q_BLOCK
    G = NSA_GQA
    nq = seq // Q
    n_var = wbias.shape[0]
    qmap = lambda b, i: (b * nq + i, 0)
    return pl.pallas_call(
        _slc_win_kernel,
        grid=(batch, nq),
        in_specs=[pl.BlockSpec((Q, NSA_W), qmap),
                  pl.BlockSpec((seq, KV_W), lambda b, i: (b, 0)),
                  pl.BlockSpec((seq, KV_W), lambda b, i: (b, 0)),
                  pl.BlockSpec((seq, KV_W), lambda b, i: (b, 1)),
                  pl.BlockSpec((seq, KV_W), lambda b, i: (b, 1)),
                  pl.BlockSpec((seq, LANES), lambda b, i: (0, 0)),
                  pl.BlockSpec((1,) + wbias.shape[1:], lambda b, i: (jnp.minimum(i, n_var - 1), 0, 0)),
                  pl.BlockSpec((1, NSA_KV_HEADS, Q, LANES), lambda b, i: (b, 0, i, 0)),
                  pl.BlockSpec((Q, LANES), qmap),
                  pl.BlockSpec((Q, NSA_W), qmap)],
        out_specs=pl.BlockSpec((Q, NSA_W), qmap),
        out_shape=jax.ShapeDtypeStruct((M, NSA_W), BF16),
        scratch_shapes=[pltpu.VMEM((NSA_KV_HEADS * G * Q, LANES), F32),
                        pltpu.VMEM((NSA_KV_HEADS * G * Q, 2 * LANES), F32)],
        compiler_params=_params(2),
        name="nsa_slc_win",
    )(q, k, v, k, v, onehot, wbias, sel, gates, o_cmp)


def _outproj_kernel(oh_ref, on_ref, og_ref, w_ref, x_ref, g_ref, lw_ref, lb_ref, o_ref, *, alpha):
    mix = (_dot(oh_ref[...], w_ref[0:HG_W, :])
           + _dot(on_ref[...], w_ref[HG_W:HG_W + NSA_W, :])
           + _dot(og_ref[...], w_ref[HG_W + NSA_W:HG_W + NSA_W + GM_W, :]))
    y = alpha * x_ref[...] + (1.0 + g_ref[0]) * mix
    o_ref[...] = _layer_norm(y, lw_ref[...], lb_ref[...])


def _out_projection(o_h, o_n, o_g, w_o, x2, g1, ln_w, ln_b, seq, alpha):
    M, D = x2.shape
    tm = ROW_TILE
    per_b = seq // tm
    row = lambda i: (i, 0)
    const = lambda i: (0, 0)
    return pl.pallas_call(
        functools.partial(_outproj_kernel, alpha=alpha),
        grid=(M // tm,),
        in_specs=[pl.BlockSpec((tm, HG_W), row),
                  pl.BlockSpec((tm, NSA_W), row),
                  pl.BlockSpec((tm, GM_W), row),
                  pl.BlockSpec(w_o.shape, const),
                  pl.BlockSpec((tm, D), row),
                  pl.BlockSpec((1, 1, D), lambda i: (i // per_b, 0, 0)),
                  pl.BlockSpec((1, D), const),
                  pl.BlockSpec((1, D), const)],
        out_specs=pl.BlockSpec((tm, D), row),
        out_shape=jax.ShapeDtypeStruct((M, D), F32),
        compiler_params=_params(1),
        name="out_proj_ln",
    )(o_h, o_n, o_g, w_o, x2, g1, ln_w, ln_b)


def _ffn_kernel(x_ref, sc_ref, sh_ref, g_ref, w1_ref, w2_ref, lw_ref, lb_ref, o_ref, *, alpha, ff_chunk):
    x = x_ref[...]
    h = (x * (1.0 + sc_ref[0]) + sh_ref[0]).astype(BF16)
    y = jnp.zeros(x.shape, F32)
    for c in range(w1_ref.shape[1] // ff_chunk):
        a = jnp.maximum(_dot(h, w1_ref[:, c * ff_chunk:(c + 1) * ff_chunk]), 0.0)
        y = y + _dot((a * a).astype(BF16), w2_ref[c * ff_chunk:(c + 1) * ff_chunk, :])
    z = alpha * x + (1.0 + g_ref[0]) * y
    o_ref[...] = _layer_norm(z, lw_ref[...], lb_ref[...])


def _ffn(x2, sc, sh, g2, w1, w2, ln_w, ln_b, seq, alpha):
    M, D = x2.shape
    tm = ROW_TILE
    per_b = seq // tm
    row = lambda i: (i, 0)
    const = lambda i: (0, 0)
    bmap = lambda i: (i // per_b, 0, 0)
    return pl.pallas_call(
        functools.partial(_ffn_kernel, alpha=alpha, ff_chunk=1024),
        grid=(M // tm,),
        in_specs=[pl.BlockSpec((tm, D), row),
                  pl.BlockSpec((1, 1, D), bmap),
                  pl.BlockSpec((1, 1, D), bmap),
                  pl.BlockSpec((1, 1, D), bmap),
                  pl.BlockSpec(w1.shape, const),
                  pl.BlockSpec(w2.shape, const),
                  pl.BlockSpec((1, D), const),
                  pl.BlockSpec((1, D), const)],
        out_specs=pl.BlockSpec((tm, D), row),
        out_shape=jax.ShapeDtypeStruct((M, D), F32),
        compiler_params=_params(1),
        name="ffn_ln",
    )(x2, sc, sh, g2, w1, w2, ln_w, ln_b)


def kernel(x, c, positions, w_in, w_o, hgrn_lower_bounds, hgrn_norm_w, cmp_pe_k, cmp_w1_k, cmp_w2_k, cmp_pe_v, cmp_w1_v, cmp_w2_v, gmlp_norm_w, gmlp_norm_b, gmlp_w_s, gmlp_b_s, w_ff1, w_ff2, w_ada, b_ada, ln1_w, ln1_b, ln2_w, ln2_b):
    B, S, D = x.shape
    L = w_in.shape[0]
    M = B * S
    alpha = (2 * L) ** 0.25
    n_seg = S // CMP_STRIDE

    inv_freq = ROPE_THETA ** (-jnp.arange(0, HEAD_DIM, 2, dtype=F32) / HEAD_DIM)
    ang = positions.astype(F32)[..., None] * inv_freq
    cos, sin = jnp.cos(ang).reshape(M, HEAD_DIM // 2), jnp.sin(ang).reshape(M, HEAD_DIM // 2)
    cos_f = jnp.concatenate([cos, cos, cos, cos], axis=-1)
    sin_s = jnp.concatenate([-sin, sin, -sin, sin], axis=-1)

    lb_sm = jax.nn.softmax(hgrn_lower_bounds.astype(F32), axis=0)
    lb_all = jnp.cumsum(lb_sm, axis=0) - lb_sm[0:1]

    col_idx, n_cols = _in_col_index()
    w_cat = jnp.where(jnp.asarray(col_idx >= 0)[None, None, :],
                      jnp.take(w_in, jnp.asarray(np.maximum(col_idx, 0)), axis=2), 0.0).astype(BF16)
    assert n_cols == w_in.shape[2]

    lane_head = np.arange(HG_W) // HEAD_DIM
    bd = jnp.asarray((lane_head[:, None] == lane_head[None, :]).astype(np.float32), dtype=BF16)
    hg_window, hg_pair_mask = _hgrn_tables()
    hg_window = jnp.asarray(hg_window, dtype=BF16)
    hg_pair_mask = jnp.asarray(hg_pair_mask, dtype=F32)
    cs = np.arange(n_seg) * CMP_STRIDE
    ss = np.arange(LANES) * SLC_BLOCK
    ov = (cs[:, None] < ss[None, :] + SLC_BLOCK) & (ss[None, :] < cs[:, None] + CMP_LEN)
    ov &= (np.arange(n_seg) < n_seg - 1)[:, None] & (np.arange(LANES) < S // SLC_BLOCK)[None, :]
    overlap = jnp.asarray(ov.T.astype(np.float32), dtype=BF16)
    assert S % SLC_KEYS == 0 and S // SLC_BLOCK <= LANES
    onehot = (jnp.arange(S)[:, None] // SLC_BLOCK == jnp.arange(LANES)[None, :]).astype(BF16)
    wbias = jnp.asarray(_window_bias())
    o_perm = HG_W + _nsa_head_perm()
    w_o = jnp.concatenate([w_o[:, :HG_W], w_o[:, o_perm], w_o[:, HG_W + NSA_W:]], axis=1)

    c8 = jnp.pad(c, ((0, 8 - B), (0, 0)))
    mod = _modulation(c8, w_ada, b_ada)[:, :B]

    x2 = x.reshape(M, D)
    for l in range(L):
        sh1, sc1, g1, sh2, sc2, g2 = [mod[l, :, i * D:(i + 1) * D].reshape(B, 1, D) for i in range(6)]
        hg, q, k, v, k_cmp, v_cmp, gm, gates = _in_projection(x2, sc1, sh1, w_cat[l], cos_f, sin_s, S)

        o_h = _hgrn2(hg, lb_all[l].reshape(1, HG_W), jnp.tile(hgrn_norm_w[l], HG_HEADS).reshape(1, HG_W), bd,
                     hg_window, hg_pair_mask, S)

        bias = jnp.repeat(gmlp_b_s[l].T, HEAD_DIM, axis=1)
        o_g = _gmlp(gm, gmlp_norm_w[l].reshape(1, GM_W), gmlp_norm_b[l].reshape(1, GM_W), gmlp_w_s[l], bias)

        pe_k = jnp.pad(cmp_pe_k[l].reshape(1, -1), ((0, 7), (0, 0)))
        pe_v = jnp.pad(cmp_pe_v[l].reshape(1, -1), ((0, 7), (0, 0)))
        seg_shape = (NSA_KV_HEADS, B, n_seg, CMP_STRIDE * HEAD_DIM)
        kc = _compress(k_cmp.reshape(seg_shape), pe_k, cmp_w1_k[l].astype(BF16), cmp_w2_k[l].astype(BF16))
        vc = _compress(v_cmp.reshape(seg_shape), pe_v, cmp_w1_v[l].astype(BF16), cmp_w2_v[l].astype(BF16))
        o_cmp, sel = _cmp_select(q, kc, vc, overlap, B, S)
        o_n = _slc_win(q, k, v, onehot, wbias, sel, gates, o_cmp, B, S)

        x2 = _out_projection(o_h, o_n, o_g, w_o[l].astype(BF16), x2, g1,
                             ln1_w[l].reshape(1, D), ln1_b[l].reshape(1, D), S, alpha)
        x2 = _ffn(x2, sc2, sh2, g2, w_ff1[l].astype(BF16), w_ff2[l].astype(BF16),
                  ln2_w[l].reshape(1, D), ln2_b[l].reshape(1, D), S, alpha)
    return x2.reshape(B, S, D)
```

```python
import functools

import numpy as np
import jax
import jax.numpy as jnp
from jax import lax
from jax.experimental import pallas as pl
from jax.experimental.pallas import tpu as pltpu

F32 = jnp.float32
BF16 = jnp.bfloat16

HEAD_DIM = 64
HG_HEADS = 4
HG_W = HG_HEADS * HEAD_DIM
NSA_HEADS = 8
NSA_W = NSA_HEADS * HEAD_DIM
NSA_GQA = 4
NSA_KV_HEADS = NSA_HEADS // NSA_GQA
KV_W = NSA_KV_HEADS * HEAD_DIM
GM_HEADS = 4
GM_W = GM_HEADS * HEAD_DIM
N_GATES = 3 * NSA_HEADS
ROPE_THETA = 10000.0
HGRN_CHUNK = 64
CMP_LEN = 32
CMP_STRIDE = 16
SLC_BLOCK = 64
SLC_TOPN = 16
WINDOW = 512
Q_BLOCK = 256
GMLP_CHUNK = 128
NEG = -1e30
BIG = 1e30
F_MIN = 1e-30
LANES = 128
ROW_TILE = 512
SLC_Q = 256
SLC_KEYS = 1024
SLC_ROW_BLOCK = 256
CMP_ROW_BLOCK = 1024
Q_SCALE = HEAD_DIM ** -0.5 * 1.4426950408889634
VMEM_LIMIT = 56 * 1024 * 1024


def _params(n_axes, semantics="parallel"):
    return pltpu.CompilerParams(dimension_semantics=(semantics,) * n_axes, vmem_limit_bytes=VMEM_LIMIT)


def _dot(a, b):
    return jnp.dot(a, b, preferred_element_type=F32)


def _dot_nt(a, b):
    return lax.dot_general(a, b, (((1,), (1,)), ((), ())), preferred_element_type=F32)


def _split2(x):
    hi = x.astype(BF16)
    lo = (x - hi.astype(F32)).astype(BF16)
    return hi, lo


def _split3(x):
    hi = x.astype(BF16)
    r = x - hi.astype(F32)
    mid = r.astype(BF16)
    lo = (r - mid.astype(F32)).astype(BF16)
    return hi, mid, lo


def _tile_lanes(x, n):
    return x if n == 1 else jnp.concatenate([x] * n, axis=1)


def _row_max(s):
    return jnp.broadcast_to(jnp.max(s, axis=-1, keepdims=True), (s.shape[0], LANES))


def _row_sum(s):
    return jnp.broadcast_to(jnp.sum(s, axis=-1, keepdims=True), (s.shape[0], LANES))


def _layer_norm(y, w, b):
    mu = jnp.mean(y, axis=-1, keepdims=True)
    d = y - mu
    var = jnp.mean(d * d, axis=-1, keepdims=True)
    return d * lax.rsqrt(var + 1e-5) * w + b


def _mod_kernel(c_ref, w_ref, b_ref, o_ref):
    a = c_ref[...]
    a = a * jax.nn.sigmoid(a)
    ah, al = _split2(a)
    wh, wl = _split2(w_ref[0])
    o_ref[0] = _dot(ah, wh) + _dot(al, wh) + _dot(ah, wl) + b_ref[0]


def _modulation(c8, w_ada, b_ada):
    L, D, N = w_ada.shape
    tn = 1024
    return pl.pallas_call(
        _mod_kernel,
        grid=(L, N // tn),
        in_specs=[pl.BlockSpec((8, D), lambda l, j: (0, 0)),
                  pl.BlockSpec((1, D, tn), lambda l, j: (l, 0, j)),
                  pl.BlockSpec((1, 1, tn), lambda l, j: (l, 0, j))],
        out_specs=pl.BlockSpec((1, 8, tn), lambda l, j: (l, 0, j)),
        out_shape=jax.ShapeDtypeStruct((L, 8, N), F32),
        compiler_params=_params(2),
        name="adaln_mod",
    )(c8, w_ada, b_ada.reshape(L, 1, N))


C_HG = 0
C_Q = C_HG + 4 * HG_W
C_K = C_Q + NSA_W
C_V = C_K + 3 * KV_W
C_GM = C_V + 3 * KV_W
C_GT = C_GM + 2 * GM_W
C_END = C_GT + LANES


def _nsa_head_perm():
    heads = [hk * NSA_GQA + g for g in range(NSA_GQA) for hk in range(NSA_KV_HEADS)]
    return np.concatenate([np.arange(h * HEAD_DIM, (h + 1) * HEAD_DIM) for h in heads])


def _in_col_index():
    o = {}
    off = 0
    for name, size in (("hq", HG_W), ("hf", HG_W), ("hi", HG_W), ("hg", HG_W), ("nq", NSA_W),
                       ("kcm", KV_W), ("vcm", KV_W), ("ksl", KV_W), ("vsl", KV_W), ("kwn", KV_W),
                       ("vwn", KV_W), ("ngt", N_GATES), ("gu", GM_W), ("gv", GM_W)):
        o[name] = np.arange(off, off + size)
        off += size
    idx = np.concatenate([o["hq"], o["hf"], o["hi"], o["hg"], o["nq"][_nsa_head_perm()],
                          o["kcm"], o["ksl"], o["kwn"], o["vcm"], o["vsl"], o["vwn"],
                          o["gu"], o["gv"], o["ngt"], np.full(LANES - N_GATES, -1)])
    return idx, off


def _rope(a, cos_f, sin_s):
    lane = lax.broadcasted_iota(jnp.int32, a.shape, 1)
    first = (lane & (HEAD_DIM - 1)) < HEAD_DIM // 2
    partner = jnp.where(first, pltpu.roll(a, LANES - HEAD_DIM // 2, 1), pltpu.roll(a, HEAD_DIM // 2, 1))
    return a * cos_f + partner * sin_s


def _inproj_kernel(x_ref, sc_ref, sh_ref, w_ref, cos_ref, sin_ref,
                   hg_ref, q_ref, k_ref, v_ref, kc_ref, vc_ref, gm_ref, gt_ref):
    h = (x_ref[...] * (1.0 + sc_ref[0]) + sh_ref[0]).astype(BF16)
    cos_f = cos_ref[...]
    sin_s = sin_ref[...]
    hg_ref[...] = _dot(h, w_ref[:, C_HG:C_Q])
    for j in range(NSA_W // LANES):
        a = _dot(h, w_ref[:, C_Q + j * LANES:C_Q + (j + 1) * LANES])
        q_ref[:, j * LANES:(j + 1) * LANES] = (_rope(a, cos_f, sin_s) * Q_SCALE).astype(BF16)
    kcm = _rope(_dot(h, w_ref[:, C_K:C_K + LANES]), cos_f, sin_s).astype(BF16)
    vcm = _dot(h, w_ref[:, C_V:C_V + LANES]).astype(BF16)
    for hk in range(NSA_KV_HEADS):
        kc_ref[hk] = kcm[:, hk * HEAD_DIM:(hk + 1) * HEAD_DIM]
        vc_ref[hk] = vcm[:, hk * HEAD_DIM:(hk + 1) * HEAD_DIM]
    for j in range(2):
        a = _dot(h, w_ref[:, C_K + (j + 1) * LANES:C_K + (j + 2) * LANES])
        k_ref[:, j * LANES:(j + 1) * LANES] = _rope(a, cos_f, sin_s).astype(BF16)
    v_ref[...] = _dot(h, w_ref[:, C_V + LANES:C_GM]).astype(BF16)
    gm_ref[...] = _dot(h, w_ref[:, C_GM:C_GT])
    gt_ref[...] = jax.nn.sigmoid(_dot(h, w_ref[:, C_GT:C_END]))


def _in_projection(x2, sc, sh, w_cat, cos_f, sin_s, seq):
    M, D = x2.shape
    tm = ROW_TILE
    per_b = seq // tm
    row = lambda i: (i, 0)
    bmap = lambda i: (i // per_b, 0, 0)
    return pl.pallas_call(
        _inproj_kernel,
        grid=(M // tm,),
        in_specs=[pl.BlockSpec((tm, D), row),
                  pl.BlockSpec((1, 1, D), bmap),
                  pl.BlockSpec((1, 1, D), bmap),
                  pl.BlockSpec((D, C_END), lambda i: (0, 0)),
                  pl.BlockSpec((tm, LANES), row),
                  pl.BlockSpec((tm, LANES), row)],
        out_specs=[pl.BlockSpec((tm, 4 * HG_W), row),
                   pl.BlockSpec((tm, NSA_W), row),
                   pl.BlockSpec((tm, 2 * KV_W), row),
                   pl.BlockSpec((tm, 2 * KV_W), row),
                   pl.BlockSpec((NSA_KV_HEADS, tm, HEAD_DIM), lambda i: (0, i, 0)),
                   pl.BlockSpec((NSA_KV_HEADS, tm, HEAD_DIM), lambda i: (0, i, 0)),
                   pl.BlockSpec((tm, 2 * GM_W), row),
                   pl.BlockSpec((tm, LANES), row)],
        out_shape=[jax.ShapeDtypeStruct((M, 4 * HG_W), F32),
                   jax.ShapeDtypeStruct((M, NSA_W), BF16),
                   jax.ShapeDtypeStruct((M, 2 * KV_W), BF16),
                   jax.ShapeDtypeStruct((M, 2 * KV_W), BF16),
                   jax.ShapeDtypeStruct((NSA_KV_HEADS, M, HEAD_DIM), BF16),
                   jax.ShapeDtypeStruct((NSA_KV_HEADS, M, HEAD_DIM), BF16),
                   jax.ShapeDtypeStruct((M, 2 * GM_W), F32),
                   jax.ShapeDtypeStruct((M, LANES), F32)],
        compiler_params=_params(1),
        name="in_proj",
    )(x2, sc, sh, w_cat, cos_f, sin_s)


HGRN_LEVELS = 6
HGRN_GROUP = 2


def _hgrn_tables():
    C = HGRN_CHUNK
    r = np.arange(C)
    wins = [r[None, :] <= r[:, None], r[None, :] > r[:, None]]
    masks = []
    for lvl in range(HGRN_LEVELS):
        c = C >> (lvl + 1)
        blk = r // c
        pair = blk // 2
        upper = blk % 2 == 1
        bound = (2 * pair + 1) * c - 1
        win_u = (r[None, :] > bound[:, None]) & (r[None, :] <= r[:, None])
        win_l = (r[None, :] > r[:, None]) & (r[None, :] <= bound[:, None])
        wins.append(np.where(upper[:, None], win_u, win_l))
        masks.append(upper[:, None] & ~upper[None, :] & (pair[:, None] == pair[None, :]))
    masks.append(np.eye(C, dtype=bool))
    window = np.concatenate(wins, axis=0).astype(np.float32)
    pair_mask = np.stack([np.tile(m, (1, HG_HEADS)) for m in masks]).astype(np.float32)
    return window, pair_mask


def _hgrn_kernel(x_ref, lb_ref, nw_ref, bd_ref, win_ref, pm_ref, o_ref, st_ref, *, steps_per_seq):
    C = HGRN_CHUNK
    W = HG_W

    @pl.when(pl.program_id(0) % steps_per_seq == 0)
    def _():
        st_ref[...] = jnp.zeros_like(st_ref)

    lb = lb_ref[...]
    nw = nw_ref[...]
    bd = bd_ref[...]
    bd_f = bd.astype(F32)
    lane_head = lax.broadcasted_iota(jnp.int32, (C, W), 1) >> 6
    own = [lane_head == h for h in range(HG_HEADS)]

    def stack(a):
        return jnp.concatenate([jnp.where(own[h], a, 0.0) for h in range(HG_HEADS)], axis=0).astype(BF16)

    def intra(r0):
        q = x_ref[pl.ds(r0, C), 0:W]
        z = x_ref[pl.ds(r0, C), W:2 * W]
        v = x_ref[pl.ds(r0, C), 2 * W:3 * W]
        f = lb + (1.0 - lb) * jax.nn.sigmoid(z)
        log_f = jnp.log(jnp.maximum(f, F_MIN))
        kk = (1.0 - lb) * jax.nn.sigmoid(-z)
        qf = q * jax.nn.sigmoid(q)
        l_hi, l_lo = _split2(log_f)
        xw = _dot(win_ref[...], jnp.concatenate([l_hi, l_lo], axis=1))
        e = jnp.exp(xw[:, 0:W] + xw[:, W:2 * W])
        e_b = e[0:C]
        attn = _dot_nt(qf.astype(BF16), stack(kk)) * pm_ref[HGRN_LEVELS]
        for lvl in range(HGRN_LEVELS):
            e_l = e[(2 + lvl) * C:(3 + lvl) * C]
            attn = attn + _dot_nt((qf * e_l).astype(BF16), stack(kk * e_l)) * pm_ref[lvl]
        o_intra = _dot(attn.astype(BF16), stack(v))
        upd = lax.dot_general(v.astype(BF16), (kk * e[C:2 * C]).astype(BF16), (((0,), (0,)), ((), ())),
                              preferred_element_type=F32) * bd_f
        return o_intra, (qf * e_b).astype(BF16), upd, e_b[C - 1:C, :]

    def finish(r0, o_intra, q_decayed, upd, e_last):
        st = st_ref[...]
        o = o_intra + _dot_nt(q_decayed, st.astype(BF16))
        st_ref[...] = st * e_last + upd
        oo = o * o
        o_hi, o_lo = _split2(oo)
        ms = (_dot(o_hi, bd) + _dot(o_lo, bd)) * (1.0 / HEAD_DIM)
        y = o * lax.rsqrt(ms + 1e-6) * nw
        g = x_ref[pl.ds(r0, C), 3 * W:4 * W]
        o_ref[pl.ds(r0, C), :] = (y * (g * jax.nn.sigmoid(g))).astype(o_ref.dtype)

    def chunk_group(ci, carry):
        rows = [pl.multiple_of(ci * (HGRN_GROUP * C) + i * C, C) for i in range(HGRN_GROUP)]
        parts = [intra(r) for r in rows]
        for r, part in zip(rows, parts):
            finish(r, *part)
        return carry

    lax.fori_loop(0, x_ref.shape[0] // (HGRN_GROUP * C), chunk_group, 0)


def _hgrn2(hg, lb, nw, bd, window, pair_mask, seq):
    M = hg.shape[0]
    tm = ROW_TILE
    return pl.pallas_call(
        functools.partial(_hgrn_kernel, steps_per_seq=seq // tm),
        grid=(M // tm,),
        in_specs=[pl.BlockSpec((tm, 4 * HG_W), lambda i: (i, 0)),
                  pl.BlockSpec((1, HG_W), lambda i: (0, 0)),
                  pl.BlockSpec((1, HG_W), lambda i: (0, 0)),
                  pl.BlockSpec((HG_W, HG_W), lambda i: (0, 0)),
                  pl.BlockSpec(window.shape, lambda i: (0, 0)),
                  pl.BlockSpec(pair_mask.shape, lambda i: (0, 0, 0))],
        out_specs=pl.BlockSpec((tm, HG_W), lambda i: (i, 0)),
        out_shape=jax.ShapeDtypeStruct((M, HG_W), BF16),
        scratch_shapes=[pltpu.VMEM((HG_W, HG_W), F32)],
        compiler_params=_params(1, "arbitrary"),
        name="hgrn2",
    )(hg, lb, nw, bd, window, pair_mask)


def _gelu(x):
    return jax.nn.gelu(x)


def _gmlp_kernel(x_ref, nw_ref, nb_ref, ws_ref, bias_ref, o_ref):
    T = GMLP_CHUNK
    u = _gelu(x_ref[:, 0:GM_W])
    v = _layer_norm(_gelu(x_ref[:, GM_W:2 * GM_W]), nw_ref[...], nb_ref[...]).astype(BF16)
    lower = lax.broadcasted_iota(jnp.int32, (T, T), 0) >= lax.broadcasted_iota(jnp.int32, (T, T), 1)
    head = lax.broadcasted_iota(jnp.int32, (T, GM_W), 1) >> 6
    bias = bias_ref[...]
    for c in range(x_ref.shape[0] // T):
        vc = v[c * T:(c + 1) * T]
        sv = bias
        for g in range(GM_HEADS):
            w = jnp.where(lower, ws_ref[g], 0.0).astype(BF16)
            sv = sv + jnp.where(head == g, _dot(w, vc), 0.0)
        o_ref[c * T:(c + 1) * T, :] = (u[c * T:(c + 1) * T] * sv).astype(o_ref.dtype)


def _gmlp(gm, nw, nb, w_s, bias):
    M = gm.shape[0]
    tm = ROW_TILE
    T = GMLP_CHUNK
    return pl.pallas_call(
        _gmlp_kernel,
        grid=(M // tm,),
        in_specs=[pl.BlockSpec((tm, 2 * GM_W), lambda i: (i, 0)),
                  pl.BlockSpec((1, GM_W), lambda i: (0, 0)),
                  pl.BlockSpec((1, GM_W), lambda i: (0, 0)),
                  pl.BlockSpec((GM_HEADS, T, T), lambda i: (0, 0, 0)),
                  pl.BlockSpec((T, GM_W), lambda i: (0, 0))],
        out_specs=pl.BlockSpec((tm, GM_W), lambda i: (i, 0)),
        out_shape=jax.ShapeDtypeStruct((M, GM_W), BF16),
        compiler_params=_params(1),
        name="gmlp",
    )(gm, nw, nb, w_s, bias)


def _compress_kernel(a_ref, pe_ref, w1_ref, w2_ref, o_ref):
    half = w1_ref.shape[0] // 2
    n_seg = a_ref.shape[2]
    a = jnp.concatenate([a_ref[hk, 0] for hk in range(NSA_KV_HEADS)], axis=0)
    u = _dot(a, w1_ref[0:half, :])
    v = _dot(a, w1_ref[half:2 * half, :])
    pe_hi, pe_lo = _split2(pe_ref[...])
    c0 = (_dot(pe_hi, w1_ref[...]) + _dot(pe_lo, w1_ref[...]))[0:1, :]
    h1 = u + pltpu.roll(v, NSA_KV_HEADS * n_seg - 1, 0) + c0
    out = _dot(_gelu(h1).astype(BF16), w2_ref[...])
    valid = (lax.broadcasted_iota(jnp.int32, out.shape, 0) & (n_seg - 1)) < n_seg - 1
    out = jnp.where(valid, out, 0.0).astype(o_ref.dtype)
    for hk in range(NSA_KV_HEADS):
        o_ref[0, :, hk * HEAD_DIM:(hk + 1) * HEAD_DIM] = out[hk * n_seg:(hk + 1) * n_seg]


def _compress(a, pe8, w1, w2):
    _, batch, n_seg, feat = a.shape
    assert n_seg & (n_seg - 1) == 0
    return pl.pallas_call(
        _compress_kernel,
        grid=(batch,),
        in_specs=[pl.BlockSpec((NSA_KV_HEADS, 1, n_seg, feat), lambda i: (0, i, 0, 0)),
                  pl.BlockSpec(pe8.shape, lambda i: (0, 0)),
                  pl.BlockSpec(w1.shape, lambda i: (0, 0)),
                  pl.BlockSpec(w2.shape, lambda i: (0, 0))],
        out_specs=pl.BlockSpec((1, n_seg, KV_W), lambda i: (i, 0, 0)),
        out_shape=jax.ShapeDtypeStruct((batch, n_seg, KV_W), BF16),
        compiler_params=_params(1),
        name="nsa_compress",
    )(a, pe8, w1, w2)


def _stacked_queries(q_ref):
    half = lax.broadcasted_iota(jnp.int32, (q_ref.shape[0], LANES), 1) >> 6
    zero = jnp.zeros((), q_ref.dtype)
    return jnp.concatenate([jnp.where(half == hk, q_ref[:, g * LANES:(g + 1) * LANES], zero)
                            for hk in range(NSA_KV_HEADS) for g in range(NSA_GQA)], axis=0)


def _cmp_select_kernel(q_ref, kc_ref, vc_ref, ov_ref, o_ref, sel_ref, o_s, *, n_blocks):
    Q = Q_BLOCK
    G = NSA_GQA
    HK = NSA_KV_HEADS
    RB = CMP_ROW_BLOCK
    n_cmp = kc_ref.shape[1]
    n_rep = n_cmp // LANES
    q0 = pl.program_id(1) * Q
    qpad = _stacked_queries(q_ref)
    kc = kc_ref[0]
    vc = vc_ref[0]
    t_row = q0 + (lax.broadcasted_iota(jnp.int32, (RB, 1), 0) & (Q - 1))
    valid = lax.broadcasted_iota(jnp.int32, (RB, n_cmp), 1) * CMP_STRIDE + (CMP_LEN - 1) <= t_row
    any_valid = jnp.broadcast_to(t_row >= CMP_LEN - 1, (RB, LANES))
    p_sum = [None] * HK
    for rb in range(HK * G * Q // RB):
        s = _dot_nt(qpad[rb * RB:(rb + 1) * RB], kc)
        sm = jnp.where(valid, s, NEG)
        e = jnp.exp2(sm - _tile_lanes(_row_max(sm), n_rep))
        p = e * _tile_lanes(jnp.where(any_valid, 1.0 / _row_sum(e), 0.0), n_rep)
        o_s[rb * RB:(rb + 1) * RB, :] = _dot(p.astype(BF16), vc)
        for i in range(RB // Q):
            hk = (rb * RB + i * Q) // (G * Q)
            blk = p[i * Q:(i + 1) * Q]
            p_sum[hk] = blk if p_sum[hk] is None else p_sum[hk] + blk
    half = lax.broadcasted_iota(jnp.int32, (Q, LANES), 1) >> 6
    for g in range(G):
        o_ref[:, g * LANES:(g + 1) * LANES] = jnp.where(half == 0, o_s[g * Q:(g + 1) * Q, :],
                                                        o_s[(G + g) * Q:(G + g + 1) * Q, :])

    ov_t = ov_ref[...]
    imp = []
    for hk in range(HK):
        p_hi, p_lo = _split2(p_sum[hk])
        imp.append(_dot_nt(ov_t, p_hi) + _dot_nt(ov_t, p_lo))
    imp = jnp.concatenate(imp, axis=1)
    j = lax.broadcasted_iota(jnp.int32, (LANES, HK * Q), 0)
    j_f = j.astype(F32)
    tb = (q0 + (lax.broadcasted_iota(jnp.int32, (LANES, HK * Q), 1) & (Q - 1))) >> 6
    forced = (j == 0) | (j == tb) | (j == tb - 1)
    score = jnp.where(j > tb, NEG, jnp.where(forced, BIG, imp))
    score = jnp.where(j < n_blocks, score, -jnp.inf)
    sel = jnp.zeros((LANES, HK * Q), F32)
    for _ in range(min(SLC_TOPN, n_blocks)):
        top = jnp.max(score, axis=0, keepdims=True)
        first = jnp.min(jnp.where(score == top, j_f, float(LANES)), axis=0, keepdims=True)
        hit = j_f == first
        sel = jnp.where(hit, 1.0, sel)
        score = jnp.where(hit, -jnp.inf, score)
    neg_sel = jnp.where(sel > 0.5, 0.0, NEG)
    for hk in range(HK):
        sel_ref[0, hk] = neg_sel[:, hk * Q:(hk + 1) * Q].T.astype(sel_ref.dtype)


def _cmp_select(q, kc, vc, overlap, batch, seq):
    M = q.shape[0]
    Q = Q_BLOCK
    nq = seq // Q
    n_cmp = kc.shape[1]
    return pl.pallas_call(
        functools.partial(_cmp_select_kernel, n_blocks=seq // SLC_BLOCK),
        grid=(batch, nq),
        in_specs=[pl.BlockSpec((Q, NSA_W), lambda b, i: (b * nq + i, 0)),
                  pl.BlockSpec((1, n_cmp, KV_W), lambda b, i: (b, 0, 0)),
                  pl.BlockSpec((1, n_cmp, KV_W), lambda b, i: (b, 0, 0)),
                  pl.BlockSpec((LANES, n_cmp), lambda b, i: (0, 0))],
        out_specs=[pl.BlockSpec((Q, NSA_W), lambda b, i: (b * nq + i, 0)),
                   pl.BlockSpec((1, NSA_KV_HEADS, Q, LANES), lambda b, i: (b, 0, i, 0))],
        out_shape=[jax.ShapeDtypeStruct((M, NSA_W), F32),
                   jax.ShapeDtypeStruct((batch, NSA_KV_HEADS, seq, LANES), BF16)],
        scratch_shapes=[pltpu.VMEM((NSA_HEADS * Q, LANES), F32)],
        compiler_params=_params(2),
        name="nsa_cmp_select",
    )(q, kc, vc, overlap)


def _window_bias():
    r = np.arange(SLC_Q)[:, None]
    j = np.arange(WINDOW + SLC_Q)[None, :]
    band = (j > r) & (j <= r + WINDOW)
    tabs = [np.where(band & (j >= WINDOW - SLC_Q * v), 0.0, NEG) for v in range(WINDOW // SLC_Q + 1)]
    return np.stack(tabs).astype(np.float32)


def _slc_win_kernel(q_ref, ks_ref, vs_ref, kw_ref, vw_ref, oh_ref, wb_ref, sel_ref, gt_ref, oc_ref, o_ref,
                    m_s, acc_s):
    Q = SLC_Q
    G = NSA_GQA
    HK = NSA_KV_HEADS
    R = HK * G * Q
    RB = SLC_ROW_BLOCK
    KC = SLC_KEYS
    NW = WINDOW // Q + 1
    q0 = pl.program_id(1) * Q
    n_full = q0 // KC
    t_row = q0 + (lax.broadcasted_iota(jnp.int32, (RB, 1), 0) & (Q - 1))
    key_off = lax.broadcasted_iota(jnp.int32, (RB, KC), 1)
    half_q = lax.broadcasted_iota(jnp.int32, (Q, LANES), 1) >> 6
    qpad = _stacked_queries(q_ref)
    qaug =jnp.concatenate([qpad, jnp.concatenate([sel_ref[0, hk] for hk in range(HK) for _ in range(G)], axis=0)],
                           axis=1)
    m_s[...] = jnp.full(m_s.shape, NEG, F32)
    acc_s[...] = jnp.zeros(acc_s.shape, F32)

    def slc_chunk(c, causal):
        k0 = pl.multiple_of(c * KC, KC)
        kaug = jnp.concatenate([ks_ref[pl.ds(k0, KC), :], oh_ref[pl.ds(k0, KC), :]], axis=1)
        vaug = jnp.concatenate([vs_ref[pl.ds(k0, KC), :], jnp.ones((KC, LANES), BF16)], axis=1)
        for rb in range(R // RB):
            rows = slice(rb * RB, (rb + 1) * RB)
            s = _dot_nt(qaug[rows], kaug)
            if causal:
                s = jnp.where(k0 + key_off <= t_row, s, NEG)
            m_old = m_s[rows, :]
            m_new = jnp.maximum(m_old, jnp.max(s, axis=-1, keepdims=True))
            p = jnp.exp2(s - _tile_lanes(m_new, KC // LANES)).astype(BF16)
            acc_s[rows, :] = _tile_lanes(jnp.exp2(m_old - m_new), 2) * acc_s[rows, :] + _dot(p, vaug)
            m_s[rows, :] = m_new

    def body(c, carry):
        slc_chunk(c, False)
        return carry

    lax.fori_loop(0, n_full, body, 0)
    slc_chunk(n_full, True)
    acc = acc_s[...]
    o_slc = acc[:, 0:LANES] / acc[:, LANES:2 * LANES]

    rows_k, rows_v = [], []
    for w in range(NW):
        k0 = pl.multiple_of(jnp.maximum(q0 - WINDOW + w * Q, 0), Q)
        rows_k.append(kw_ref[pl.ds(k0, Q), :])
        rows_v.append(vw_ref[pl.ds(k0, Q), :])
    k_win = jnp.concatenate(rows_k, axis=0)
    v_win = jnp.concatenate([jnp.concatenate(rows_v, axis=0), jnp.ones((NW * Q, LANES), BF16)], axis=1)
    s = _dot_nt(qpad, k_win) + jnp.concatenate([wb_ref[0]] * (HK * G), axis=0)
    p = jnp.exp2(s - _tile_lanes(_row_max(s), NW * Q // LANES)).astype(BF16)
    acc = _dot(p, v_win)
    o_win = acc[:, 0:LANES] / acc[:, LANES:2 * LANES]

    gates = gt_ref[...]
    for g in range(G):
        oc = oc_ref[:, g * LANES:(g + 1) * LANES]
        tiles = []
        for hk in range(HK):
            c0 = 3 * (hk * G + g)
            rows = slice((hk * G + g) * Q, (hk * G + g + 1) * Q)
            tiles.append(gates[:, c0:c0 + 1] * oc + gates[:, c0 + 1:c0 + 2] * o_slc[rows]
                         + gates[:, c0 + 2:c0 + 3] * o_win[rows])
        o_ref[:, g * LANES:(g + 1) * LANES] = jnp.where(half_q == 0, tiles[0], tiles[1]).astype(o_ref.dtype)


def _slc_win(q, k, v, onehot, wbias, sel, gates, o_cmp, batch, seq):
    M = q.shape[0]
    Q = SLC_Q
    G = NSA_GQA
    nq = seq // Q
    n_var = wbias.shape[0]
    qmap = lambda b, i: (b * nq + i, 0)
    return pl.pallas_call(
        _slc_win_kernel,
        grid=(batch, nq),
        in_specs=[pl.BlockSpec((Q, NSA_W), qmap),
                  pl.BlockSpec((seq, KV_W), lambda b, i: (b, 0)),
                  pl.BlockSpec((seq, KV_W), lambda b, i: (b, 0)),
                  pl.BlockSpec((seq, KV_W), lambda b, i: (b, 1)),
                  pl.BlockSpec((seq, KV_W), lambda b, i: (b, 1)),
                  pl.BlockSpec((seq, LANES), lambda b, i: (0, 0)),
                  pl.BlockSpec((1,) + wbias.shape[1:], lambda b, i: (jnp.minimum(i, n_var - 1), 0, 0)),
                  pl.BlockSpec((1, NSA_KV_HEADS, Q, LANES), lambda b, i: (b, 0, i, 0)),
                  pl.BlockSpec((Q, LANES), qmap),
                  pl.BlockSpec((Q, NSA_W), qmap)],
        out_specs=pl.BlockSpec((Q, NSA_W), qmap),
        out_shape=jax.ShapeDtypeStruct((M, NSA_W), BF16),
        scratch_shapes=[pltpu.VMEM((NSA_KV_HEADS * G * Q, LANES), F32),
                        pltpu.VMEM((NSA_KV_HEADS * G * Q, 2 * LANES), F32)],
        compiler_params=_params(2),
        name="nsa_slc_win",
    )(q, k, v, k, v, onehot, wbias, sel, gates, o_cmp)


def _outproj_kernel(oh_ref, on_ref, og_ref, w_ref, x_ref, g_ref, lw_ref, lb_ref, o_ref, *, alpha):
    mix = (_dot(oh_ref[...], w_ref[0:HG_W, :])
           + _dot(on_ref[...], w_ref[HG_W:HG_W + NSA_W, :])
           + _dot(og_ref[...], w_ref[HG_W + NSA_W:HG_W + NSA_W + GM_W, :]))
    y = alpha * x_ref[...] + (1.0 + g_ref[0]) * mix
    o_ref[...] = _layer_norm(y, lw_ref[...], lb_ref[...])


def _out_projection(o_h, o_n, o_g, w_o, x2, g1, ln_w, ln_b, seq, alpha):
    M, D = x2.shape
    tm = ROW_TILE
    per_b = seq // tm
    row = lambda i: (i, 0)
    const = lambda i: (0, 0)
    return pl.pallas_call(
        functools.partial(_outproj_kernel, alpha=alpha),
        grid=(M // tm,),
        in_specs=[pl.BlockSpec((tm, HG_W), row),
                  pl.BlockSpec((tm, NSA_W), row),
                  pl.BlockSpec((tm, GM_W), row),
                  pl.BlockSpec(w_o.shape, const),
                  pl.BlockSpec((tm, D), row),
                  pl.BlockSpec((1, 1, D), lambda i: (i // per_b, 0, 0)),
                  pl.BlockSpec((1, D), const),
                  pl.BlockSpec((1, D), const)],
        out_specs=pl.BlockSpec((tm, D), row),
        out_shape=jax.ShapeDtypeStruct((M, D), F32),
        compiler_params=_params(1),
        name="out_proj_ln",
    )(o_h, o_n, o_g, w_o, x2, g1, ln_w, ln_b)


def _ffn_kernel(x_ref, sc_ref, sh_ref, g_ref, w1_ref, w2_ref, lw_ref, lb_ref, o_ref, *, alpha, ff_chunk):
    x = x_ref[...]
    h = (x * (1.0 + sc_ref[0]) + sh_ref[0]).astype(BF16)
    y = jnp.zeros(x.shape, F32)
    for c in range(w1_ref.shape[1] // ff_chunk):
        a = jnp.maximum(_dot(h, w1_ref[:, c * ff_chunk:(c + 1) * ff_chunk]), 0.0)
        y = y + _dot((a * a).astype(BF16), w2_ref[c * ff_chunk:(c + 1) * ff_chunk, :])
    z = alpha * x + (1.0 + g_ref[0]) * y
    o_ref[...] = _layer_norm(z, lw_ref[...], lb_ref[...])


def _ffn(x2, sc, sh, g2, w1, w2, ln_w, ln_b, seq, alpha):
    M, D = x2.shape
    tm = ROW_TILE
    per_b = seq // tm
    row = lambda i: (i, 0)
    const = lambda i: (0, 0)
    bmap = lambda i: (i // per_b, 0, 0)
    return pl.pallas_call(
        functools.partial(_ffn_kernel, alpha=alpha, ff_chunk=1024),
        grid=(M // tm,),
        in_specs=[pl.BlockSpec((tm, D), row),
                  pl.BlockSpec((1, 1, D), bmap),
                  pl.BlockSpec((1, 1, D), bmap),
                  pl.BlockSpec((1, 1, D), bmap),
                  pl.BlockSpec(w1.shape, const),
                  pl.BlockSpec(w2.shape, const),
                  pl.BlockSpec((1, D), const),
                  pl.BlockSpec((1, D), const)],
        out_specs=pl.BlockSpec((tm, D), row),
        out_shape=jax.ShapeDtypeStruct((M, D), F32),
        compiler_params=_params(1),
        name="ffn_ln",
    )(x2, sc, sh, g2, w1, w2, ln_w, ln_b)


def kernel(x, c, positions, w_in, w_o, hgrn_lower_bounds, hgrn_norm_w, cmp_pe_k, cmp_w1_k, cmp_w2_k, cmp_pe_v, cmp_w1_v, cmp_w2_v, gmlp_norm_w, gmlp_norm_b, gmlp_w_s, gmlp_b_s, w_ff1, w_ff2, w_ada, b_ada, ln1_w, ln1_b, ln2_w, ln2_b):
    B, S, D = x.shape
    L = w_in.shape[0]
    M = B * S
    alpha = (2 * L) ** 0.25
    n_seg = S // CMP_STRIDE

    inv_freq = ROPE_THETA ** (-jnp.arange(0, HEAD_DIM, 2, dtype=F32) / HEAD_DIM)
    ang = positions.astype(F32)[..., None] * inv_freq
    cos, sin = jnp.cos(ang).reshape(M, HEAD_DIM // 2), jnp.sin(ang).reshape(M, HEAD_DIM // 2)
    cos_f = jnp.concatenate([cos, cos, cos, cos], axis=-1)
    sin_s = jnp.concatenate([-sin, sin, -sin, sin], axis=-1)

    lb_sm = jax.nn.softmax(hgrn_lower_bounds.astype(F32), axis=0)
    lb_all = jnp.cumsum(lb_sm, axis=0) - lb_sm[0:1]

    col_idx, n_cols = _in_col_index()
    assert n_cols == w_in.shape[2]
    step = np.where(col_idx >= 0, 1, 0)
    cuts = [0] + [i for i in range(1, len(col_idx)) if col_idx[i] != col_idx[i - 1] + step[i]] + [len(col_idx)]
    pieces = [w_in[:, :, col_idx[a]:col_idx[b - 1] + 1].astype(BF16) if col_idx[a] >= 0
              else jnp.zeros(w_in.shape[:2] + (b - a,), BF16) for a, b in zip(cuts[:-1], cuts[1:])]
    w_cat = jnp.concatenate(pieces, axis=2)

    lane_head = np.arange(HG_W) // HEAD_DIM
    bd = jnp.asarray((lane_head[:, None] == lane_head[None, :]).astype(np.float32), dtype=BF16)
    hg_window, hg_pair_mask = _hgrn_tables()
    hg_window = jnp.asarray(hg_window, dtype=BF16)
    hg_pair_mask = jnp.asarray(hg_pair_mask, dtype=F32)
    cs = np.arange(n_seg) * CMP_STRIDE
    ss = np.arange(LANES) * SLC_BLOCK
    ov = (cs[:, None] < ss[None, :] + SLC_BLOCK) & (ss[None, :] < cs[:, None] + CMP_LEN)
    ov &= (np.arange(n_seg) < n_seg - 1)[:, None] & (np.arange(LANES) < S // SLC_BLOCK)[None, :]
    overlap = jnp.asarray(ov.T.astype(np.float32), dtype=BF16)
    assert S % SLC_KEYS == 0 and S // SLC_BLOCK <= LANES
    onehot = (jnp.arange(S)[:, None] // SLC_BLOCK == jnp.arange(LANES)[None, :]).astype(BF16)
    wbias = jnp.asarray(_window_bias())
    o_perm = HG_W + _nsa_head_perm()
    w_o = jnp.concatenate([w_o[:, :HG_W], w_o[:, o_perm], w_o[:, HG_W + NSA_W:]], axis=1)

    c8 = jnp.pad(c, ((0, 8 - B), (0, 0)))
    mod = _modulation(c8, w_ada, b_ada)[:, :B]

    x2 = x.reshape(M, D)
    for l in range(L):
        sh1, sc1, g1, sh2, sc2, g2 = [mod[l, :, i * D:(i + 1) * D].reshape(B, 1, D) for i in range(6)]
        hg, q, k, v, k_cmp, v_cmp, gm, gates = _in_projection(x2, sc1, sh1, w_cat[l], cos_f, sin_s, S)

        o_h = _hgrn2(hg, lb_all[l].reshape(1, HG_W), jnp.tile(hgrn_norm_w[l], HG_HEADS).reshape(1, HG_W), bd,
                     hg_window, hg_pair_mask, S)

        bias = jnp.repeat(gmlp_b_s[l].T, HEAD_DIM, axis=1)
        o_g = _gmlp(gm, gmlp_norm_w[l].reshape(1, GM_W), gmlp_norm_b[l].reshape(1, GM_W), gmlp_w_s[l], bias)

        pe_k = jnp.pad(cmp_pe_k[l].reshape(1, -1), ((0, 7), (0, 0)))
        pe_v = jnp.pad(cmp_pe_v[l].reshape(1, -1), ((0, 7), (0, 0)))
        seg_shape = (NSA_KV_HEADS, B, n_seg, CMP_STRIDE * HEAD_DIM)
        kc = _compress(k_cmp.reshape(seg_shape), pe_k, cmp_w1_k[l].astype(BF16), cmp_w2_k[l].astype(BF16))
        vc = _compress(v_cmp.reshape(seg_shape), pe_v, cmp_w1_v[l].astype(BF16), cmp_w2_v[l].astype(BF16))
        o_cmp, sel = _cmp_select(q, kc, vc, overlap, B, S)
        o_n = _slc_win(q, k, v, onehot, wbias, sel, gates, o_cmp, B, S)

        x2 = _out_projection(o_h, o_n, o_g, w_o[l].astype(BF16), x2, g1,
                             ln1_w[l].reshape(1, D), ln1_b[l].reshape(1, D), S, alpha)
        x2 = _ffn(x2, sc2, sh2, g2, w_ff1[l].astype(BF16), w_ff2[l].astype(BF16),
                  ln2_w[l].reshape(1, D), ln2_b[l].reshape(1, D), S, alpha)
    return x2.reshape(B, S, D)
```

```python
import functools

import numpy as np
import jax
import jax.numpy as jnp
from jax import lax
from jax.experimental import pallas as pl
from jax.experimental.pallas import tpu as pltpu

F32 = jnp.float32
BF16 = jnp.bfloat16

HEAD_DIM = 64
HG_HEADS = 4
HG_W = HG_HEADS * HEAD_DIM
NSA_HEADS = 8
NSA_W = NSA_HEADS * HEAD_DIM
NSA_GQA = 4
NSA_KV_HEADS = NSA_HEADS // NSA_GQA
KV_W = NSA_KV_HEADS * HEAD_DIM
GM_HEADS = 4
GM_W = GM_HEADS * HEAD_DIM
N_GATES = 3 * NSA_HEADS
ROPE_THETA = 10000.0
HGRN_CHUNK = 64
CMP_LEN = 32
CMP_STRIDE = 16
SLC_BLOCK = 64
SLC_TOPN = 16
WINDOW = 512
Q_BLOCK = 256
GMLP_CHUNK = 128
NEG = -1e30
BIG = 1e30
F_MIN = 1e-30
LANES = 128
ROW_TILE = 512
SLC_Q = 256
SLC_KEYS = 512
SLC_GROUP = 4
SLC_ROW_BLOCK = 256
CMP_ROW_BLOCK = 1024
Q_SCALE = HEAD_DIM ** -0.5 * 1.4426950408889634
VMEM_LIMIT = 56 * 1024 * 1024


def _params(n_axes, semantics="parallel"):
    return pltpu.CompilerParams(dimension_semantics=(semantics,) * n_axes, vmem_limit_bytes=VMEM_LIMIT)


def _dot(a, b):
    return jnp.dot(a, b, preferred_element_type=F32)


def _dot_nt(a, b):
    return lax.dot_general(a, b, (((1,), (1,)), ((), ())), preferred_element_type=F32)


def _split2(x):
    hi = x.astype(BF16)
    lo = (x - hi.astype(F32)).astype(BF16)
    return hi, lo


def _split3(x):
    hi = x.astype(BF16)
    r = x - hi.astype(F32)
    mid = r.astype(BF16)
    lo = (r - mid.astype(F32)).astype(BF16)
    return hi, mid, lo


def _tile_lanes(x, n):
    return x if n == 1 else jnp.concatenate([x] * n, axis=1)


def _row_max(s):
    return jnp.broadcast_to(jnp.max(s, axis=-1, keepdims=True), (s.shape[0], LANES))


def _row_sum(s):
    return jnp.broadcast_to(jnp.sum(s, axis=-1, keepdims=True), (s.shape[0], LANES))


def _layer_norm(y, w, b):
    mu = jnp.mean(y, axis=-1, keepdims=True)
    d = y - mu
    var = jnp.mean(d * d, axis=-1, keepdims=True)
    return d * lax.rsqrt(var + 1e-5) * w + b


def _mod_kernel(c_ref, w_ref, b_ref, o_ref):
    a = c_ref[...]
    a = a * jax.nn.sigmoid(a)
    ah, al = _split2(a)
    wh, wl = _split2(w_ref[0])
    o_ref[0] = _dot(ah, wh) + _dot(al, wh) + _dot(ah, wl) + b_ref[0]


def _modulation(c8, w_ada, b_ada):
    L, D, N = w_ada.shape
    tn = 1024
    return pl.pallas_call(
        _mod_kernel,
        grid=(L, N // tn),
        in_specs=[pl.BlockSpec((8, D), lambda l, j: (0, 0)),
                  pl.BlockSpec((1, D, tn), lambda l, j: (l, 0, j)),
                  pl.BlockSpec((1, 1, tn), lambda l, j: (l, 0, j))],
        out_specs=pl.BlockSpec((1, 8, tn), lambda l, j: (l, 0, j)),
        out_shape=jax.ShapeDtypeStruct((L, 8, N), F32),
        compiler_params=_params(2),
        name="adaln_mod",
    )(c8, w_ada, b_ada.reshape(L, 1, N))


C_HG = 0
C_Q = C_HG + 4 * HG_W
C_K = C_Q + NSA_W
C_V = C_K + 3 * KV_W
C_GM = C_V + 3 * KV_W
C_GT = C_GM + 2 * GM_W
C_END = C_GT + LANES


def _nsa_head_perm():
    heads = [hk * NSA_GQA + g for g in range(NSA_GQA) for hk in range(NSA_KV_HEADS)]
    return np.concatenate([np.arange(h * HEAD_DIM, (h + 1) * HEAD_DIM) for h in heads])


def _in_col_index():
    o = {}
    off = 0
    for name, size in (("hq", HG_W), ("hf", HG_W), ("hi", HG_W), ("hg", HG_W), ("nq", NSA_W),
                       ("kcm", KV_W), ("vcm", KV_W), ("ksl", KV_W), ("vsl", KV_W), ("kwn", KV_W),
                       ("vwn", KV_W), ("ngt", N_GATES), ("gu", GM_W), ("gv", GM_W)):
        o[name] = np.arange(off, off + size)
        off += size
    idx = np.concatenate([o["hq"], o["hf"], o["hi"], o["hg"], o["nq"][_nsa_head_perm()],
                          o["kcm"], o["ksl"], o["kwn"], o["vcm"], o["vsl"], o["vwn"],
                          o["gu"], o["gv"], o["ngt"], np.full(LANES - N_GATES, -1)])
    return idx, off


def _rope(a, cos_f, sin_s):
    lane = lax.broadcasted_iota(jnp.int32, a.shape, 1)
    first = (lane & (HEAD_DIM - 1)) < HEAD_DIM // 2
    partner = jnp.where(first, pltpu.roll(a, LANES - HEAD_DIM // 2, 1), pltpu.roll(a, HEAD_DIM // 2, 1))
    return a * cos_f + partner * sin_s


def _inproj_kernel(x_ref, sc_ref, sh_ref, w_ref, cos_ref, sin_ref,
                   hg_ref, q_ref, k_ref, v_ref, kc_ref, vc_ref, gm_ref, gt_ref):
    h = (x_ref[...] * (1.0 + sc_ref[0]) + sh_ref[0]).astype(BF16)
    cos_f = cos_ref[...]
    sin_s = sin_ref[...]
    y = _dot(h, w_ref[...])
    tile = lambda c: y[:, c:c + LANES]
    hg_ref[...] = y[:, C_HG:C_Q]
    for j in range(NSA_W // LANES):
        q_ref[:, j * LANES:(j + 1) * LANES] = (_rope(tile(C_Q + j * LANES), cos_f, sin_s) * Q_SCALE).astype(BF16)
    kcm = _rope(tile(C_K), cos_f, sin_s).astype(BF16)
    vcm = tile(C_V).astype(BF16)
    for hk in range(NSA_KV_HEADS):
        kc_ref[hk] = kcm[:, hk * HEAD_DIM:(hk + 1) * HEAD_DIM]
        vc_ref[hk] = vcm[:, hk * HEAD_DIM:(hk + 1) * HEAD_DIM]
    for j in range(2):
        k_ref[:, j * LANES:(j + 1) * LANES] = _rope(tile(C_K + (j + 1) * LANES), cos_f, sin_s).astype(BF16)
    v_ref[...] = y[:, C_V + LANES:C_GM].astype(BF16)
    gm_ref[...] = y[:, C_GM:C_GT]
    gt_ref[...] = jax.nn.sigmoid(tile(C_GT))


def _in_projection(x2, sc, sh, w_cat, cos_f, sin_s, seq):
    M, D = x2.shape
    tm = ROW_TILE
    per_b = seq // tm
    row = lambda i: (i, 0)
    bmap = lambda i: (i // per_b, 0, 0)
    return pl.pallas_call(
        _inproj_kernel,
        grid=(M // tm,),
        in_specs=[pl.BlockSpec((tm, D), row),
                  pl.BlockSpec((1, 1, D), bmap),
                  pl.BlockSpec((1, 1, D), bmap),
                  pl.BlockSpec((D, C_END), lambda i: (0, 0)),
                  pl.BlockSpec((tm, LANES), row),
                  pl.BlockSpec((tm, LANES), row)],
        out_specs=[pl.BlockSpec((tm, 4 * HG_W), row),
                   pl.BlockSpec((tm, NSA_W), row),
                   pl.BlockSpec((tm, 2 * KV_W), row),
                   pl.BlockSpec((tm, 2 * KV_W), row),
                   pl.BlockSpec((NSA_KV_HEADS, tm, HEAD_DIM), lambda i: (0, i, 0)),
                   pl.BlockSpec((NSA_KV_HEADS, tm, HEAD_DIM), lambda i: (0, i, 0)),
                   pl.BlockSpec((tm, 2 * GM_W), row),
                   pl.BlockSpec((tm, LANES), row)],
        out_shape=[jax.ShapeDtypeStruct((M, 4 * HG_W), F32),
                   jax.ShapeDtypeStruct((M, NSA_W), BF16),
                   jax.ShapeDtypeStruct((M, 2 * KV_W), BF16),
                   jax.ShapeDtypeStruct((M, 2 * KV_W), BF16),
                   jax.ShapeDtypeStruct((NSA_KV_HEADS, M, HEAD_DIM), BF16),
                   jax.ShapeDtypeStruct((NSA_KV_HEADS, M, HEAD_DIM), BF16),
                   jax.ShapeDtypeStruct((M, 2 * GM_W), F32),
                   jax.ShapeDtypeStruct((M, LANES), F32)],
        compiler_params=_params(1),
        name="in_proj",
    )(x2, sc, sh, w_cat, cos_f, sin_s)


HGRN_LEVELS = 6
HGRN_GROUP = 2


def _hgrn_tables():
    C = HGRN_CHUNK
    r = np.arange(C)
    wins = [r[None, :] <= r[:, None], r[None, :] > r[:, None]]
    masks = []
    for lvl in range(HGRN_LEVELS):
        c = C >> (lvl + 1)
        blk = r // c
        pair = blk // 2
        upper = blk % 2 == 1
        bound = (2 * pair + 1) * c - 1
        win_u = (r[None, :] > bound[:, None]) & (r[None, :] <= r[:, None])
        win_l = (r[None, :] > r[:, None]) & (r[None, :] <= bound[:, None])
        wins.append(np.where(upper[:, None], win_u, win_l))
        masks.append(upper[:, None] & ~upper[None, :] & (pair[:, None] == pair[None, :]))
    masks.append(np.eye(C, dtype=bool))
    window = np.concatenate(wins, axis=0).astype(np.float32)
    pair_mask = np.stack([np.tile(m, (1, HG_HEADS)) for m in masks]).astype(np.float32)
    return window, pair_mask


def _hgrn_kernel(x_ref, lb_ref, nw_ref, bd_ref, win_ref, pm_ref, o_ref, st_ref, *, steps_per_seq):
    C = HGRN_CHUNK
    W = HG_W

    @pl.when(pl.program_id(0) % steps_per_seq == 0)
    def _():
        st_ref[...] = jnp.zeros_like(st_ref)

    lb = lb_ref[...]
    nw = nw_ref[...]
    bd = bd_ref[...]
    bd_f = bd.astype(F32)
    lane_head = lax.broadcasted_iota(jnp.int32, (C, W), 1) >> 6
    own = [lane_head == h for h in range(HG_HEADS)]

    def stack(a):
        return jnp.concatenate([jnp.where(own[h], a, 0.0) for h in range(HG_HEADS)], axis=0).astype(BF16)

    def intra(r0):
        q = x_ref[pl.ds(r0, C), 0:W]
        z = x_ref[pl.ds(r0, C), W:2 * W]
        v = x_ref[pl.ds(r0, C), 2 * W:3 * W]
        f = lb + (1.0 - lb) * jax.nn.sigmoid(z)
        log_f = jnp.log(jnp.maximum(f, F_MIN))
        kk = (1.0 - lb) * jax.nn.sigmoid(-z)
        qf = q * jax.nn.sigmoid(q)
        l_hi, l_lo = _split2(log_f)
        xw = _dot(win_ref[...], jnp.concatenate([l_hi, l_lo], axis=1))
        e = jnp.exp(xw[:, 0:W] + xw[:, W:2 * W])
        e_b = e[0:C]
        attn = _dot_nt(qf.astype(BF16), stack(kk)) * pm_ref[HGRN_LEVELS]
        for lvl in range(HGRN_LEVELS):
            e_l = e[(2 + lvl) * C:(3 + lvl) * C]
            attn = attn + _dot_nt((qf * e_l).astype(BF16), stack(kk * e_l)) * pm_ref[lvl]
        o_intra = _dot(attn.astype(BF16), stack(v))
        upd = lax.dot_general(v.astype(BF16), (kk * e[C:2 * C]).astype(BF16), (((0,), (0,)), ((), ())),
                              preferred_element_type=F32) * bd_f
        return o_intra, (qf * e_b).astype(BF16), upd, e_b[C - 1:C, :]

    def finish(r0, o_intra, q_decayed, upd, e_last):
        st = st_ref[...]
        o = o_intra + _dot_nt(q_decayed, st.astype(BF16))
        st_ref[...] = st * e_last + upd
        oo = o * o
        o_hi, o_lo = _split2(oo)
        ms = (_dot(o_hi, bd) + _dot(o_lo, bd)) * (1.0 / HEAD_DIM)
        y = o * lax.rsqrt(ms + 1e-6) * nw
        g = x_ref[pl.ds(r0, C), 3 * W:4 * W]
        o_ref[pl.ds(r0, C), :] = (y * (g * jax.nn.sigmoid(g))).astype(o_ref.dtype)

    def chunk_group(ci, carry):
        rows = [pl.multiple_of(ci * (HGRN_GROUP * C) + i * C, C) for i in range(HGRN_GROUP)]
        parts = [intra(r) for r in rows]
        for r, part in zip(rows, parts):
            finish(r, *part)
        return carry

    lax.fori_loop(0, x_ref.shape[0] // (HGRN_GROUP * C), chunk_group, 0)


def _hgrn2(hg, lb, nw, bd, window, pair_mask, seq):
    M = hg.shape[0]
    tm = ROW_TILE
    return pl.pallas_call(
        functools.partial(_hgrn_kernel, steps_per_seq=seq // tm),
        grid=(M // tm,),
        in_specs=[pl.BlockSpec((tm, 4 * HG_W), lambda i: (i, 0)),
                  pl.BlockSpec((1, HG_W), lambda i: (0, 0)),
                  pl.BlockSpec((1, HG_W), lambda i: (0, 0)),
                  pl.BlockSpec((HG_W, HG_W), lambda i: (0, 0)),
                  pl.BlockSpec(window.shape, lambda i: (0, 0)),
                  pl.BlockSpec(pair_mask.shape, lambda i: (0, 0, 0))],
        out_specs=pl.BlockSpec((tm, HG_W), lambda i: (i, 0)),
        out_shape=jax.ShapeDtypeStruct((M, HG_W), BF16),
        scratch_shapes=[pltpu.VMEM((HG_W, HG_W), F32)],
        compiler_params=_params(1, "arbitrary"),
        name="hgrn2",
    )(hg, lb, nw, bd, window, pair_mask)


def _gelu(x):
    return jax.nn.gelu(x)


def _gmlp_kernel(x_ref, nw_ref, nb_ref, ws_ref, bias_ref, o_ref):
    T = GMLP_CHUNK
    u = _gelu(x_ref[:, 0:GM_W])
    v = _layer_norm(_gelu(x_ref[:, GM_W:2 * GM_W]), nw_ref[...], nb_ref[...]).astype(BF16)
    lower = lax.broadcasted_iota(jnp.int32, (T, T), 0) >= lax.broadcasted_iota(jnp.int32, (T, T), 1)
    head = lax.broadcasted_iota(jnp.int32, (T, GM_W), 1) >> 6
    bias = bias_ref[...]
    for c in range(x_ref.shape[0] // T):
        vc = v[c * T:(c + 1) * T]
        sv = bias
        for g in range(GM_HEADS):
            w = jnp.where(lower, ws_ref[g], 0.0).astype(BF16)
            sv = sv + jnp.where(head == g, _dot(w, vc), 0.0)
        o_ref[c * T:(c + 1) * T, :] = (u[c * T:(c + 1) * T] * sv).astype(o_ref.dtype)


def _gmlp(gm, nw, nb, w_s, bias):
    M = gm.shape[0]
    tm = ROW_TILE
    T = GMLP_CHUNK
    return pl.pallas_call(
        _gmlp_kernel,
        grid=(M // tm,),
        in_specs=[pl.BlockSpec((tm, 2 * GM_W), lambda i: (i, 0)),
                  pl.BlockSpec((1, GM_W), lambda i: (0, 0)),
                  pl.BlockSpec((1, GM_W), lambda i: (0, 0)),
                  pl.BlockSpec((GM_HEADS, T, T), lambda i: (0, 0, 0)),
                  pl.BlockSpec((T, GM_W), lambda i: (0, 0))],
        out_specs=pl.BlockSpec((tm, GM_W), lambda i: (i, 0)),
        out_shape=jax.ShapeDtypeStruct((M, GM_W), BF16),
        compiler_params=_params(1),
        name="gmlp",
    )(gm, nw, nb, w_s, bias)


def _compress_kernel(a_ref, pe_ref, w1_ref, w2_ref, o_ref):
    half = w1_ref.shape[0] // 2
    n_seg = a_ref.shape[2]
    a = jnp.concatenate([a_ref[hk, 0] for hk in range(NSA_KV_HEADS)], axis=0)
    u = _dot(a, w1_ref[0:half, :])
    v = _dot(a, w1_ref[half:2 * half, :])
    pe_hi, pe_lo = _split2(pe_ref[...])
    c0 = (_dot(pe_hi, w1_ref[...]) + _dot(pe_lo, w1_ref[...]))[0:1, :]
    h1 = u + pltpu.roll(v, NSA_KV_HEADS * n_seg - 1, 0) + c0
    out = _dot(_gelu(h1).astype(BF16), w2_ref[...])
    valid = (lax.broadcasted_iota(jnp.int32, out.shape, 0) & (n_seg - 1)) < n_seg - 1
    out = jnp.where(valid, out, 0.0).astype(o_ref.dtype)
    for hk in range(NSA_KV_HEADS):
        o_ref[0, :, hk * HEAD_DIM:(hk + 1) * HEAD_DIM] = out[hk * n_seg:(hk + 1) * n_seg]


def _compress(a, pe8, w1, w2):
    _, batch, n_seg, feat = a.shape
    assert n_seg & (n_seg - 1) == 0
    return pl.pallas_call(
        _compress_kernel,
        grid=(batch,),
        in_specs=[pl.BlockSpec((NSA_KV_HEADS, 1, n_seg, feat), lambda i: (0, i, 0, 0)),
                  pl.BlockSpec(pe8.shape, lambda i: (0, 0)),
                  pl.BlockSpec(w1.shape, lambda i: (0, 0)),
                  pl.BlockSpec(w2.shape, lambda i: (0, 0))],
        out_specs=pl.BlockSpec((1, n_seg, KV_W), lambda i: (i, 0, 0)),
        out_shape=jax.ShapeDtypeStruct((batch, n_seg, KV_W), BF16),
        compiler_params=_params(1),
        name="nsa_compress",
    )(a, pe8, w1, w2)


def _stacked_queries(q_ref):
    half = lax.broadcasted_iota(jnp.int32, (q_ref.shape[0], LANES), 1) >> 6
    zero = jnp.zeros((), q_ref.dtype)
    return jnp.concatenate([jnp.where(half == hk, q_ref[:, g * LANES:(g + 1) * LANES], zero)
                            for hk in range(NSA_KV_HEADS) for g in range(NSA_GQA)], axis=0)


def _cmp_select_kernel(q_ref, kc_ref, vc_ref, ov_ref, o_ref, sel_ref, o_s, *, n_blocks):
    Q = Q_BLOCK
    G = NSA_GQA
    HK = NSA_KV_HEADS
    RB = CMP_ROW_BLOCK
    n_cmp = kc_ref.shape[1]
    n_rep = n_cmp // LANES
    q0 = pl.program_id(1) * Q
    qpad = _stacked_queries(q_ref)
    kc = kc_ref[0]
    vc = vc_ref[0]
    t_row = q0 + (lax.broadcasted_iota(jnp.int32, (RB, 1), 0) & (Q - 1))
    valid = lax.broadcasted_iota(jnp.int32, (RB, n_cmp), 1) * CMP_STRIDE + (CMP_LEN - 1) <= t_row
    any_valid = jnp.broadcast_to(t_row >= CMP_LEN - 1, (RB, LANES))
    p_sum = [None] * HK
    for rb in range(HK * G * Q // RB):
        s = _dot_nt(qpad[rb * RB:(rb + 1) * RB], kc)
        sm = jnp.where(valid, s, NEG)
        e = jnp.exp2(sm - _tile_lanes(_row_max(sm), n_rep))
        p = e * _tile_lanes(jnp.where(any_valid, 1.0 / _row_sum(e), 0.0), n_rep)
        o_s[rb * RB:(rb + 1) * RB, :] = _dot(p.astype(BF16), vc)
        for i in range(RB // Q):
            hk = (rb * RB + i * Q) // (G * Q)
            blk = p[i * Q:(i + 1) * Q]
            p_sum[hk] = blk if p_sum[hk] is None else p_sum[hk] + blk
    half = lax.broadcasted_iota(jnp.int32, (Q, LANES), 1) >> 6
    for g in range(G):
        o_ref[:, g * LANES:(g + 1) * LANES] = jnp.where(half == 0, o_s[g * Q:(g + 1) * Q, :],
                                                        o_s[(G + g) * Q:(G + g + 1) * Q, :])

    ov_t = ov_ref[...]
    imp = []
    for hk in range(HK):
        p_hi, p_lo = _split2(p_sum[hk])
        imp.append(_dot_nt(ov_t, p_hi) + _dot_nt(ov_t, p_lo))
    imp = jnp.concatenate(imp, axis=1)

    def choose(rows):
        j = lax.broadcasted_iota(jnp.int32, (rows, HK * Q), 0)
        j_f = j.astype(F32)
        tb = (q0 + (lax.broadcasted_iota(jnp.int32, (rows, HK * Q), 1) & (Q - 1))) >> 6
        forced = (j == 0) | (j == tb) | (j == tb - 1)
        score = jnp.where(j > tb, NEG, jnp.where(forced, BIG, imp[0:rows]))
        score = jnp.where(j < n_blocks, score, -jnp.inf)
        for _ in range(min(SLC_TOPN, n_blocks)):
            top = jnp.max(score, axis=0, keepdims=True)
            first = jnp.min(jnp.where(score == top, j_f, float(LANES)), axis=0, keepdims=True)
            score = jnp.where(j_f == first, -jnp.inf, score)
        neg_sel = jnp.where((score == -jnp.inf) & (j < n_blocks), 0.0, NEG)
        if rows < LANES:
            neg_sel = jnp.concatenate([neg_sel, jnp.full((LANES - rows, HK * Q), NEG, F32)], axis=0)
        for hk in range(HK):
            sel_ref[0, hk] = neg_sel[:, hk * Q:(hk + 1) * Q].T.astype(sel_ref.dtype)

    few = q0 + Q <= (LANES // 2) * SLC_BLOCK

    @pl.when(few)
    def _():
        choose(LANES // 2)

    @pl.when(jnp.logical_not(few))
    def _():
        choose(LANES)


def _cmp_select(q, kc, vc, overlap, batch, seq):
    M = q.shape[0]
    Q = Q_BLOCK
    nq = seq // Q
    n_cmp = kc.shape[1]
    return pl.pallas_call(
        functools.partial(_cmp_select_kernel, n_blocks=seq // SLC_BLOCK),
        grid=(batch, nq),
        in_specs=[pl.BlockSpec((Q, NSA_W), lambda b, i: (b * nq + i, 0)),
                  pl.BlockSpec((1, n_cmp, KV_W), lambda b, i: (b, 0, 0)),
                  pl.BlockSpec((1, n_cmp, KV_W), lambda b, i: (b, 0, 0)),
                  pl.BlockSpec((LANES, n_cmp), lambda b, i: (0, 0))],
        out_specs=[pl.BlockSpec((Q, NSA_W), lambda b, i: (b * nq + i, 0)),
                   pl.BlockSpec((1, NSA_KV_HEADS, Q, LANES), lambda b, i: (b, 0, i, 0))],
        out_shape=[jax.ShapeDtypeStruct((M, NSA_W), F32),
                   jax.ShapeDtypeStruct((batch, NSA_KV_HEADS, seq, LANES), BF16)],
        scratch_shapes=[pltpu.VMEM((NSA_HEADS * Q, LANES), F32)],
        compiler_params=_params(2),
        name="nsa_cmp_select",
    )(q, kc, vc, overlap)


def _window_bias():
    r = np.arange(SLC_Q)[:, None]
    j = np.arange(WINDOW + SLC_Q)[None, :]
    band = (j > r) & (j <= r + WINDOW)
    tabs = [np.where(band & (j >= WINDOW - SLC_Q * v), 0.0, NEG) for v in range(WINDOW // SLC_Q + 1)]
    return np.stack(tabs).astype(np.float32)


def _slc_win_kernel(q_ref, ks_ref, vs_ref, kw_ref, vw_ref, oh_ref, wb_ref, sel_ref, gt_ref, oc_ref, o_ref,
                    m_s, acc_s):
    Q = SLC_Q
    G = NSA_GQA
    HK = NSA_KV_HEADS
    R = HK * G * Q
    RB = SLC_ROW_BLOCK
    KC = SLC_KEYS
    NW = WINDOW // Q + 1
    q0 = pl.program_id(1) * Q
    n_full = q0 // KC
    t_row = q0 + (lax.broadcasted_iota(jnp.int32, (RB, 1), 0) & (Q - 1))
    key_off = lax.broadcasted_iota(jnp.int32, (RB, KC), 1)
    half_q = lax.broadcasted_iota(jnp.int32, (Q, LANES), 1) >> 6
    qpad = _stacked_queries(q_ref)
    qaug =jnp.concatenate([qpad, jnp.concatenate([sel_ref[0, hk] for hk in range(HK) for _ in range(G)], axis=0)],
                           axis=1)
    m_s[...] = jnp.full(m_s.shape, NEG, F32)
    acc_s[...] = jnp.zeros(acc_s.shape, F32)

    def slc_chunk(c, causal):
        k0 = pl.multiple_of(c * KC, KC)
        kaug = jnp.concatenate([ks_ref[pl.ds(k0, KC), :], oh_ref[pl.ds(k0, KC), :]], axis=1)
        vaug = jnp.concatenate([vs_ref[pl.ds(k0, KC), :], jnp.ones((KC, LANES), BF16)], axis=1)
        for rb in range(R // RB):
            rows = slice(rb * RB, (rb + 1) * RB)
            s = _dot_nt(qaug[rows], kaug)
            if causal:
                s = jnp.where(k0 + key_off <= t_row, s, NEG)
            m_old = m_s[rows, :]
            m_new = jnp.maximum(m_old, jnp.max(s, axis=-1, keepdims=True))
            p = jnp.exp2(s - _tile_lanes(m_new, KC // LANES)).astype(BF16)
            acc_s[rows, :] = _tile_lanes(jnp.exp2(m_old - m_new), 2) * acc_s[rows, :] + _dot(p, vaug)
            m_s[rows, :] = m_new

    def run(c0, n):
        for i in range(n):
            slc_chunk(c0 + i, False)

    def body(i, carry):
        run(i * SLC_GROUP, SLC_GROUP)
        return carry

    lax.fori_loop(0, n_full // SLC_GROUP, body, 0)
    base = n_full - n_full % SLC_GROUP
    size = SLC_GROUP // 2
    while size >= 1:
        rest = n_full % (2 * size)

        @pl.when(rest >= size)
        def _(base=base, size=size):
            run(base, size)

        base = base + jnp.where(rest >= size, size, 0)
        size //= 2
    slc_chunk(n_full, True)
    acc = acc_s[...]
    o_slc = acc[:, 0:LANES] / acc[:, LANES:2 * LANES]

    rows_k, rows_v = [], []
    for w in range(NW):
        k0 = pl.multiple_of(jnp.maximum(q0 - WINDOW + w * Q, 0), Q)
        rows_k.append(kw_ref[pl.ds(k0, Q), :])
        rows_v.append(vw_ref[pl.ds(k0, Q), :])
    k_win = jnp.concatenate(rows_k, axis=0)
    v_win = jnp.concatenate([jnp.concatenate(rows_v, axis=0), jnp.ones((NW * Q, LANES), BF16)], axis=1)
    s = _dot_nt(qpad, k_win) + jnp.concatenate([wb_ref[0]] * (HK * G), axis=0)
    p = jnp.exp2(s - _tile_lanes(_row_max(s), NW * Q // LANES)).astype(BF16)
    acc = _dot(p, v_win)
    o_win = acc[:, 0:LANES] / acc[:, LANES:2 * LANES]

    gates = gt_ref[...]
    for g in range(G):
        oc = oc_ref[:, g * LANES:(g + 1) * LANES]
        tiles = []
        for hk in range(HK):
            c0 = 3 * (hk * G + g)
            rows = slice((hk * G + g) * Q, (hk * G + g + 1) * Q)
            tiles.append(gates[:, c0:c0 + 1] * oc + gates[:, c0 + 1:c0 + 2] * o_slc[rows]
                         + gates[:, c0 + 2:c0 + 3] * o_win[rows])
        o_ref[:, g * LANES:(g + 1) * LANES] = jnp.where(half_q == 0, tiles[0], tiles[1]).astype(o_ref.dtype)


def _slc_win(q, k, v, onehot, wbias, sel, gates, o_cmp, batch, seq):
    M = q.shape[0]
    Q = SLC_Q
    G = NSA_GQA
    nq = seq // Q
    n_var = wbias.shape[0]
    qmap = lambda b, i: (b * nq + i, 0)
    return pl.pallas_call(
        _slc_win_kernel,
        grid=(batch, nq),
        in_specs=[pl.BlockSpec((Q, NSA_W), qmap),
                  pl.BlockSpec((seq, KV_W), lambda b, i: (b, 0)),
                  pl.BlockSpec((seq, KV_W), lambda b, i: (b, 0)),
                  pl.BlockSpec((seq, KV_W), lambda b, i: (b, 1)),
                  pl.BlockSpec((seq, KV_W), lambda b, i: (b, 1)),
                  pl.BlockSpec((seq, LANES), lambda b, i: (0, 0)),
                  pl.BlockSpec((1,) + wbias.shape[1:], lambda b, i: (jnp.minimum(i, n_var - 1), 0, 0)),
                  pl.BlockSpec((1, NSA_KV_HEADS, Q, LANES), lambda b, i: (b, 0, i, 0)),
                  pl.BlockSpec((Q, LANES), qmap),
                  pl.BlockSpec((Q, NSA_W), qmap)],
        out_specs=pl.BlockSpec((Q, NSA_W), qmap),
        out_shape=jax.ShapeDtypeStruct((M, NSA_W), BF16),
        scratch_shapes=[pltpu.VMEM((NSA_KV_HEADS * G * Q, LANES), F32),
                        pltpu.VMEM((NSA_KV_HEADS * G * Q, 2 * LANES), F32)],
        compiler_params=_params(2),
        name="nsa_slc_win",
    )(q, k, v, k, v, onehot, wbias, sel, gates, o_cmp)


def _outproj_kernel(oh_ref, on_ref, og_ref, w_ref, x_ref, g_ref, lw_ref, lb_ref, o_ref, *, alpha):
    mix = (_dot(oh_ref[...], w_ref[0:HG_W, :])
           + _dot(on_ref[...], w_ref[HG_W:HG_W + NSA_W, :])
           + _dot(og_ref[...], w_ref[HG_W + NSA_W:HG_W + NSA_W + GM_W, :]))
    y = alpha * x_ref[...] + (1.0 + g_ref[0]) * mix
    o_ref[...] = _layer_norm(y, lw_ref[...], lb_ref[...])


def _out_projection(o_h, o_n, o_g, w_o, x2, g1, ln_w, ln_b, seq, alpha):
    M, D = x2.shape
    tm = ROW_TILE
    per_b = seq // tm
    row = lambda i: (i, 0)
    const = lambda i: (0, 0)
    return pl.pallas_call(
        functools.partial(_outproj_kernel, alpha=alpha),
        grid=(M // tm,),
        in_specs=[pl.BlockSpec((tm, HG_W), row),
                  pl.BlockSpec((tm, NSA_W), row),
                  pl.BlockSpec((tm, GM_W), row),
                  pl.BlockSpec(w_o.shape, const),
                  pl.BlockSpec((tm, D), row),
                  pl.BlockSpec((1, 1, D), lambda i: (i // per_b, 0, 0)),
                  pl.BlockSpec((1, D), const),
                  pl.BlockSpec((1, D), const)],
        out_specs=pl.BlockSpec((tm, D), row),
        out_shape=jax.ShapeDtypeStruct((M, D), F32),
        compiler_params=_params(1),
        name="out_proj_ln",
    )(o_h, o_n, o_g, w_o, x2, g1, ln_w, ln_b)


def _ffn_kernel(x_ref, sc_ref, sh_ref, g_ref, w1_ref, w2_ref, lw_ref, lb_ref, o_ref, *, alpha, ff_chunk):
    x = x_ref[...]
    h = (x * (1.0 + sc_ref[0]) + sh_ref[0]).astype(BF16)
    y = jnp.zeros(x.shape, F32)
    for c in range(w1_ref.shape[1] // ff_chunk):
        a = jnp.maximum(_dot(h, w1_ref[:, c * ff_chunk:(c + 1) * ff_chunk]), 0.0)
        y = y + _dot((a * a).astype(BF16), w2_ref[c * ff_chunk:(c + 1) * ff_chunk, :])
    z = alpha * x + (1.0 + g_ref[0]) * y
    o_ref[...] = _layer_norm(z, lw_ref[...], lb_ref[...])


def _ffn(x2, sc, sh, g2, w1, w2, ln_w, ln_b, seq, alpha):
    M, D = x2.shape
    tm = ROW_TILE
    per_b = seq // tm
    row = lambda i: (i, 0)
    const = lambda i: (0, 0)
    bmap = lambda i: (i // per_b, 0, 0)
    return pl.pallas_call(
        functools.partial(_ffn_kernel, alpha=alpha, ff_chunk=1024),
        grid=(M // tm,),
        in_specs=[pl.BlockSpec((tm, D), row),
                  pl.BlockSpec((1, 1, D), bmap),
                  pl.BlockSpec((1, 1, D), bmap),
                  pl.BlockSpec((1, 1, D), bmap),
                  pl.BlockSpec(w1.shape, const),
                  pl.BlockSpec(w2.shape, const),
                  pl.BlockSpec((1, D), const),
                  pl.BlockSpec((1, D), const)],
        out_specs=pl.BlockSpec((tm, D), row),
        out_shape=jax.ShapeDtypeStruct((M, D), F32),
        compiler_params=_params(1),
        name="ffn_ln",
    )(x2, sc, sh, g2, w1, w2, ln_w, ln_b)


def kernel(x, c, positions, w_in, w_o, hgrn_lower_bounds, hgrn_norm_w, cmp_pe_k, cmp_w1_k, cmp_w2_k, cmp_pe_v, cmp_w1_v, cmp_w2_v, gmlp_norm_w, gmlp_norm_b, gmlp_w_s, gmlp_b_s, w_ff1, w_ff2, w_ada, b_ada, ln1_w, ln1_b, ln2_w, ln2_b):
    B, S, D = x.shape
    L = w_in.shape[0]
    M = B * S
    alpha = (2 * L) ** 0.25
    n_seg = S // CMP_STRIDE

    inv_freq = ROPE_THETA ** (-jnp.arange(0, HEAD_DIM, 2, dtype=F32) / HEAD_DIM)
    ang = positions.astype(F32)[..., None] * inv_freq
    cos, sin = jnp.cos(ang).reshape(M, HEAD_DIM // 2), jnp.sin(ang).reshape(M, HEAD_DIM // 2)
    cos_f = jnp.concatenate([cos, cos, cos, cos], axis=-1)
    sin_s = jnp.concatenate([-sin, sin, -sin, sin], axis=-1)

    lb_sm = jax.nn.softmax(hgrn_lower_bounds.astype(F32), axis=0)
    lb_all = jnp.cumsum(lb_sm, axis=0) - lb_sm[0:1]

    col_idx, n_cols = _in_col_index()
    assert n_cols == w_in.shape[2]
    step = np.where(col_idx >= 0, 1, 0)
    cuts = [0] + [i for i in range(1, len(col_idx)) if col_idx[i] != col_idx[i - 1] + step[i]] + [len(col_idx)]
    pieces = [w_in[:, :, col_idx[a]:col_idx[b - 1] + 1].astype(BF16) if col_idx[a] >= 0
              else jnp.zeros(w_in.shape[:2] + (b - a,), BF16) for a, b in zip(cuts[:-1], cuts[1:])]
    w_cat = jnp.concatenate(pieces, axis=2)

    lane_head = np.arange(HG_W) // HEAD_DIM
    bd = jnp.asarray((lane_head[:, None] == lane_head[None, :]).astype(np.float32), dtype=BF16)
    hg_window, hg_pair_mask = _hgrn_tables()
    hg_window = jnp.asarray(hg_window, dtype=BF16)
    hg_pair_mask = jnp.asarray(hg_pair_mask, dtype=F32)
    cs = np.arange(n_seg) * CMP_STRIDE
    ss = np.arange(LANES) * SLC_BLOCK
    ov = (cs[:, None] < ss[None, :] + SLC_BLOCK) & (ss[None, :] < cs[:, None] + CMP_LEN)
    ov &= (np.arange(n_seg) < n_seg - 1)[:, None] & (np.arange(LANES) < S // SLC_BLOCK)[None, :]
    overlap = jnp.asarray(ov.T.astype(np.float32), dtype=BF16)
    assert S % SLC_KEYS == 0 and S // SLC_BLOCK <= LANES
    onehot = (jnp.arange(S)[:, None] // SLC_BLOCK == jnp.arange(LANES)[None, :]).astype(BF16)
    wbias = jnp.asarray(_window_bias())
    o_perm = HG_W + _nsa_head_perm()
    w_o = jnp.concatenate([w_o[:, :HG_W], w_o[:, o_perm], w_o[:, HG_W + NSA_W:]], axis=1)

    c8 = jnp.pad(c, ((0, 8 - B), (0, 0)))
    mod = _modulation(c8, w_ada, b_ada)[:, :B]

    x2 = x.reshape(M, D)
    for l in range(L):
        sh1, sc1, g1, sh2, sc2, g2 = [mod[l, :, i * D:(i + 1) * D].reshape(B, 1, D) for i in range(6)]
        hg, q, k, v, k_cmp, v_cmp, gm, gates = _in_projection(x2, sc1, sh1, w_cat[l], cos_f, sin_s, S)

        o_h = _hgrn2(hg, lb_all[l].reshape(1, HG_W), jnp.tile(hgrn_norm_w[l], HG_HEADS).reshape(1, HG_W), bd,
                     hg_window, hg_pair_mask, S)

        bias = jnp.repeat(gmlp_b_s[l].T, HEAD_DIM, axis=1)
        o_g = _gmlp(gm, gmlp_norm_w[l].reshape(1, GM_W), gmlp_norm_b[l].reshape(1, GM_W), gmlp_w_s[l], bias)

        pe_k = jnp.pad(cmp_pe_k[l].reshape(1, -1), ((0, 7), (0, 0)))
        pe_v = jnp.pad(cmp_pe_v[l].reshape(1, -1), ((0, 7), (0, 0)))
        seg_shape = (NSA_KV_HEADS, B, n_seg, CMP_STRIDE * HEAD_DIM)
        kc = _compress(k_cmp.reshape(seg_shape), pe_k, cmp_w1_k[l].astype(BF16), cmp_w2_k[l].astype(BF16))
        vc = _compress(v_cmp.reshape(seg_shape), pe_v, cmp_w1_v[l].astype(BF16), cmp_w2_v[l].astype(BF16))
        o_cmp, sel = _cmp_select(q, kc, vc, overlap, B, S)
        o_n = _slc_win(q, k, v, onehot, wbias, sel, gates, o_cmp, B, S)

        x2 = _out_projection(o_h, o_n, o_g, w_o[l].astype(BF16), x2, g1,
                             ln1_w[l].reshape(1, D), ln1_b[l].reshape(1, D), S, alpha)
        x2 = _ffn(x2, sc2, sh2, g2, w_ff1[l].astype(BF16), w_ff2[l].astype(BF16),
                  ln2_w[l].reshape(1, D), ln2_b[l].reshape(1, D), S, alpha)
    return x2.reshape(B, S, D)
```

```python
import functools

import numpy as np
import jax
import jax.numpy as jnp
from jax import lax
from jax.experimental import pallas as pl
from jax.experimental.pallas import tpu as pltpu

F32 = jnp.float32
BF16 = jnp.bfloat16

HEAD_DIM = 64
HG_HEADS = 4
HG_W = HG_HEADS * HEAD_DIM
NSA_HEADS = 8
NSA_W = NSA_HEADS * HEAD_DIM
NSA_GQA = 4
NSA_KV_HEADS = NSA_HEADS // NSA_GQA
KV_W = NSA_KV_HEADS * HEAD_DIM
GM_HEADS = 4
GM_W = GM_HEADS * HEAD_DIM
N_GATES = 3 * NSA_HEADS
ROPE_THETA = 10000.0
HGRN_CHUNK = 64
CMP_LEN = 32
CMP_STRIDE = 16
SLC_BLOCK = 64
SLC_TOPN = 16
WINDOW = 512
Q_BLOCK = 256
GMLP_CHUNK = 128
NEG = -1e30
BIG = 1e30
F_MIN = 1e-30
LANES = 128
ROW_TILE = 512
SLC_Q = 256
SLC_KEYS = 512
SLC_GROUP = 4
SLC_ROW_BLOCK = 512
CMP_ROW_BLOCK = 1024
Q_SCALE = HEAD_DIM ** -0.5 * 1.4426950408889634
VMEM_LIMIT = 56 * 1024 * 1024


def _params(n_axes, semantics="parallel"):
    return pltpu.CompilerParams(dimension_semantics=(semantics,) * n_axes, vmem_limit_bytes=VMEM_LIMIT)


def _dot(a, b):
    return jnp.dot(a, b, preferred_element_type=F32)


def _dot_nt(a, b):
    return lax.dot_general(a, b, (((1,), (1,)), ((), ())), preferred_element_type=F32)


def _split2(x):
    hi = x.astype(BF16)
    lo = (x - hi.astype(F32)).astype(BF16)
    return hi, lo


def _split3(x):
    hi = x.astype(BF16)
    r = x - hi.astype(F32)
    mid = r.astype(BF16)
    lo = (r - mid.astype(F32)).astype(BF16)
    return hi, mid, lo


def _tile_lanes(x, n):
    return x if n == 1 else jnp.concatenate([x] * n, axis=1)


def _row_max(s):
    return jnp.broadcast_to(jnp.max(s, axis=-1, keepdims=True), (s.shape[0], LANES))


def _row_sum(s):
    return jnp.broadcast_to(jnp.sum(s, axis=-1, keepdims=True), (s.shape[0], LANES))


def _layer_norm(y, w, b):
    mu = jnp.mean(y, axis=-1, keepdims=True)
    d = y - mu
    var = jnp.mean(d * d, axis=-1, keepdims=True)
    return d * lax.rsqrt(var + 1e-5) * w + b


def _mod_kernel(c_ref, w_ref, b_ref, o_ref):
    a = c_ref[...]
    a = a * jax.nn.sigmoid(a)
    ah, al = _split2(a)
    wh, wl = _split2(w_ref[0])
    o_ref[0] = _dot(ah, wh) + _dot(al, wh) + _dot(ah, wl) + b_ref[0]


def _modulation(c8, w_ada, b_ada):
    L, D, N = w_ada.shape
    tn = 1024
    return pl.pallas_call(
        _mod_kernel,
        grid=(L, N // tn),
        in_specs=[pl.BlockSpec((8, D), lambda l, j: (0, 0)),
                  pl.BlockSpec((1, D, tn), lambda l, j: (l, 0, j)),
                  pl.BlockSpec((1, 1, tn), lambda l, j: (l, 0, j))],
        out_specs=pl.BlockSpec((1, 8, tn), lambda l, j: (l, 0, j)),
        out_shape=jax.ShapeDtypeStruct((L, 8, N), F32),
        compiler_params=_params(2),
        name="adaln_mod",
    )(c8, w_ada, b_ada.reshape(L, 1, N))


C_HG = 0
C_Q = C_HG + 4 * HG_W
C_K = C_Q + NSA_W
C_V = C_K + 3 * KV_W
C_GM = C_V + 3 * KV_W
C_GT = C_GM + 2 * GM_W
C_END = C_GT + LANES


def _nsa_head_perm():
    heads = [hk * NSA_GQA + g for g in range(NSA_GQA) for hk in range(NSA_KV_HEADS)]
    return np.concatenate([np.arange(h * HEAD_DIM, (h + 1) * HEAD_DIM) for h in heads])


def _in_col_index():
    o = {}
    off = 0
    for name, size in (("hq", HG_W), ("hf", HG_W), ("hi", HG_W), ("hg", HG_W), ("nq", NSA_W),
                       ("kcm", KV_W), ("vcm", KV_W), ("ksl", KV_W), ("vsl", KV_W), ("kwn", KV_W),
                       ("vwn", KV_W), ("ngt", N_GATES), ("gu", GM_W), ("gv", GM_W)):
        o[name] = np.arange(off, off + size)
        off += size
    idx = np.concatenate([o["hq"], o["hf"], o["hi"], o["hg"], o["nq"][_nsa_head_perm()],
                          o["kcm"], o["ksl"], o["kwn"], o["vcm"], o["vsl"], o["vwn"],
                          o["gu"], o["gv"], o["ngt"], np.full(LANES - N_GATES, -1)])
    return idx, off


def _rope(a, cos_f, sin_s):
    lane = lax.broadcasted_iota(jnp.int32, a.shape, 1)
    first = (lane & (HEAD_DIM - 1)) < HEAD_DIM // 2
    partner = jnp.where(first, pltpu.roll(a, LANES - HEAD_DIM // 2, 1), pltpu.roll(a, HEAD_DIM // 2, 1))
    return a * cos_f + partner * sin_s


def _inproj_kernel(x_ref, sc_ref, sh_ref, w_ref, cos_ref, sin_ref,
                   hg_ref, q_ref, k_ref, v_ref, kc_ref, vc_ref, gm_ref, gt_ref, seg_s):
    h = (x_ref[...] * (1.0 + sc_ref[0]) + sh_ref[0]).astype(BF16)
    cos_f = cos_ref[...]
    sin_s = sin_ref[...]
    y = _dot(h, w_ref[...])
    tile = lambda c: y[:, c:c + LANES]
    hg_ref[...] = y[:, C_HG:C_Q]
    for j in range(NSA_W // LANES):
        q_ref[:, j * LANES:(j + 1) * LANES] = (_rope(tile(C_Q + j * LANES), cos_f, sin_s) * Q_SCALE).astype(BF16)
    seg_s[0] = _rope(tile(C_K), cos_f, sin_s)
    seg_s[1] = tile(C_V)
    n_seg = x_ref.shape[0] // CMP_STRIDE
    for p in range(CMP_STRIDE):
        for src, dst in ((0, kc_ref), (1, vc_ref)):
            rows = seg_s[src, pl.ds(p, n_seg, stride=CMP_STRIDE), :].astype(BF16)
            for hk in range(NSA_KV_HEADS):
                dst[hk, :, p * HEAD_DIM:(p + 1) * HEAD_DIM] = rows[:, hk * HEAD_DIM:(hk + 1) * HEAD_DIM]
    for j in range(2):
        k_ref[:, j * LANES:(j + 1) * LANES] = _rope(tile(C_K + (j + 1) * LANES), cos_f, sin_s).astype(BF16)
    v_ref[...] = y[:, C_V + LANES:C_GM].astype(BF16)
    gm_ref[...] = y[:, C_GM:C_GT]
    gt_ref[...] = jax.nn.sigmoid(tile(C_GT))


def _in_projection(x2, sc, sh, w_cat, cos_f, sin_s, seq):
    M, D = x2.shape
    tm = ROW_TILE
    per_b = seq // tm
    seg_w = CMP_STRIDE * HEAD_DIM
    row = lambda i: (i, 0)
    bmap = lambda i: (i // per_b, 0, 0)
    return pl.pallas_call(
        _inproj_kernel,
        grid=(M // tm,),
        in_specs=[pl.BlockSpec((tm, D), row),
                  pl.BlockSpec((1, 1, D), bmap),
                  pl.BlockSpec((1, 1, D), bmap),
                  pl.BlockSpec((D, C_END), lambda i: (0, 0)),
                  pl.BlockSpec((tm, LANES), row),
                  pl.BlockSpec((tm, LANES), row)],
        out_specs=[pl.BlockSpec((tm, 4 * HG_W), row),
                   pl.BlockSpec((tm, NSA_W), row),
                   pl.BlockSpec((tm, 2 * KV_W), row),
                   pl.BlockSpec((tm, 2 * KV_W), row),
                   pl.BlockSpec((NSA_KV_HEADS, tm // CMP_STRIDE, seg_w), lambda i: (0, i, 0)),
                   pl.BlockSpec((NSA_KV_HEADS, tm // CMP_STRIDE, seg_w), lambda i: (0, i, 0)),
                   pl.BlockSpec((tm, 2 * GM_W), row),
                   pl.BlockSpec((tm, LANES), row)],
        out_shape=[jax.ShapeDtypeStruct((M, 4 * HG_W), F32),
                   jax.ShapeDtypeStruct((M, NSA_W), BF16),
                   jax.ShapeDtypeStruct((M, 2 * KV_W), BF16),
                   jax.ShapeDtypeStruct((M, 2 * KV_W), BF16),
                   jax.ShapeDtypeStruct((NSA_KV_HEADS, M // CMP_STRIDE, seg_w), BF16),
                   jax.ShapeDtypeStruct((NSA_KV_HEADS, M // CMP_STRIDE, seg_w), BF16),
                   jax.ShapeDtypeStruct((M, 2 * GM_W), F32),
                   jax.ShapeDtypeStruct((M, LANES), F32)],
        scratch_shapes=[pltpu.VMEM((2, tm, LANES), F32)],
        compiler_params=_params(1),
        name="in_proj",
    )(x2, sc, sh, w_cat, cos_f, sin_s)


HGRN_LEVELS = 6
HGRN_GROUP = 2


def _hgrn_tables():
    C = HGRN_CHUNK
    r = np.arange(C)
    wins = [r[None, :] <= r[:, None], r[None, :] > r[:, None]]
    masks = []
    for lvl in range(HGRN_LEVELS):
        c = C >> (lvl + 1)
        blk = r // c
        pair = blk // 2
        upper = blk % 2 == 1
        bound = (2 * pair + 1) * c - 1
        win_u = (r[None, :] > bound[:, None]) & (r[None, :] <= r[:, None])
        win_l = (r[None, :] > r[:, None]) & (r[None, :] <= bound[:, None])
        wins.append(np.where(upper[:, None], win_u, win_l))
        masks.append(upper[:, None] & ~upper[None, :] & (pair[:, None] == pair[None, :]))
    masks.append(np.eye(C, dtype=bool))
    window = np.concatenate(wins, axis=0).astype(np.float32)
    pair_mask = np.stack([np.tile(m, (1, HG_HEADS)) for m in masks]).astype(np.float32)
    return window, pair_mask


def _hgrn_kernel(x_ref, lb_ref, nw_ref, bd_ref, win_ref, pm_ref, o_ref, st_ref, *, steps_per_seq):
    C = HGRN_CHUNK
    W = HG_W

    @pl.when(pl.program_id(0) % steps_per_seq == 0)
    def _():
        st_ref[...] = jnp.zeros_like(st_ref)

    lb = lb_ref[...]
    nw = nw_ref[...]
    bd = bd_ref[...]
    bd_f = bd.astype(F32)
    lane_head = lax.broadcasted_iota(jnp.int32, (C, W), 1) >> 6
    own = [lane_head == h for h in range(HG_HEADS)]

    def stack(a):
        return jnp.concatenate([jnp.where(own[h], a, 0.0) for h in range(HG_HEADS)], axis=0).astype(BF16)

    def intra(r0):
        q = x_ref[pl.ds(r0, C), 0:W]
        z = x_ref[pl.ds(r0, C), W:2 * W]
        v = x_ref[pl.ds(r0, C), 2 * W:3 * W]
        f = lb + (1.0 - lb) * jax.nn.sigmoid(z)
        log_f = jnp.log(jnp.maximum(f, F_MIN))
        kk = (1.0 - lb) * jax.nn.sigmoid(-z)
        qf = q * jax.nn.sigmoid(q)
        l_hi, l_lo = _split2(log_f)
        xw = _dot(win_ref[...], jnp.concatenate([l_hi, l_lo], axis=1))
        e = jnp.exp(xw[:, 0:W] + xw[:, W:2 * W])
        e_b = e[0:C]
        attn = _dot_nt(qf.astype(BF16), stack(kk)) * pm_ref[HGRN_LEVELS]
        for lvl in range(HGRN_LEVELS):
            e_l = e[(2 + lvl) * C:(3 + lvl) * C]
            attn = attn + _dot_nt((qf * e_l).astype(BF16), stack(kk * e_l)) * pm_ref[lvl]
        o_intra = _dot(attn.astype(BF16), stack(v))
        upd = lax.dot_general(v.astype(BF16), (kk * e[C:2 * C]).astype(BF16), (((0,), (0,)), ((), ())),
                              preferred_element_type=F32) * bd_f
        return o_intra, (qf * e_b).astype(BF16), upd, e_b[C - 1:C, :]

    def finish(r0, o_intra, q_decayed, upd, e_last):
        st = st_ref[...]
        o = o_intra + _dot_nt(q_decayed, st.astype(BF16))
        st_ref[...] = st * e_last + upd
        oo = o * o
        o_hi, o_lo = _split2(oo)
        ms = (_dot(o_hi, bd) + _dot(o_lo, bd)) * (1.0 / HEAD_DIM)
        y = o * lax.rsqrt(ms + 1e-6) * nw
        g = x_ref[pl.ds(r0, C), 3 * W:4 * W]
        o_ref[pl.ds(r0, C), :] = (y * (g * jax.nn.sigmoid(g))).astype(o_ref.dtype)

    def chunk_group(ci, carry):
        rows = [pl.multiple_of(ci * (HGRN_GROUP * C) + i * C, C) for i in range(HGRN_GROUP)]
        parts = [intra(r) for r in rows]
        for r, part in zip(rows, parts):
            finish(r, *part)
        return carry

    lax.fori_loop(0, x_ref.shape[0] // (HGRN_GROUP * C), chunk_group, 0)


def _hgrn2(hg, lb, nw, bd, window, pair_mask, seq):
    M = hg.shape[0]
    tm = ROW_TILE
    return pl.pallas_call(
        functools.partial(_hgrn_kernel, steps_per_seq=seq // tm),
        grid=(M // tm,),
        in_specs=[pl.BlockSpec((tm, 4 * HG_W), lambda i: (i, 0)),
                  pl.BlockSpec((1, HG_W), lambda i: (0, 0)),
                  pl.BlockSpec((1, HG_W), lambda i: (0, 0)),
                  pl.BlockSpec((HG_W, HG_W), lambda i: (0, 0)),
                  pl.BlockSpec(window.shape, lambda i: (0, 0)),
                  pl.BlockSpec(pair_mask.shape, lambda i: (0, 0, 0))],
        out_specs=pl.BlockSpec((tm, HG_W), lambda i: (i, 0)),
        out_shape=jax.ShapeDtypeStruct((M, HG_W), BF16),
        scratch_shapes=[pltpu.VMEM((HG_W, HG_W), F32)],
        compiler_params=_params(1, "arbitrary"),
        name="hgrn2",
    )(hg, lb, nw, bd, window, pair_mask)


def _gelu(x):
    return jax.nn.gelu(x)


def _gmlp_kernel(x_ref, nw_ref, nb_ref, ws_ref, bias_ref, o_ref):
    T = GMLP_CHUNK
    u = _gelu(x_ref[:, 0:GM_W])
    v = _layer_norm(_gelu(x_ref[:, GM_W:2 * GM_W]), nw_ref[...], nb_ref[...]).astype(BF16)
    lower = lax.broadcasted_iota(jnp.int32, (T, T), 0) >= lax.broadcasted_iota(jnp.int32, (T, T), 1)
    head = lax.broadcasted_iota(jnp.int32, (T, GM_W), 1) >> 6
    bias = bias_ref[...]
    for c in range(x_ref.shape[0] // T):
        vc = v[c * T:(c + 1) * T]
        sv = bias
        for g in range(GM_HEADS):
            w = jnp.where(lower, ws_ref[g], 0.0).astype(BF16)
            sv = sv + jnp.where(head == g, _dot(w, vc), 0.0)
        o_ref[c * T:(c + 1) * T, :] = (u[c * T:(c + 1) * T] * sv).astype(o_ref.dtype)


def _gmlp(gm, nw, nb, w_s, bias):
    M = gm.shape[0]
    tm = ROW_TILE
    T = GMLP_CHUNK
    return pl.pallas_call(
        _gmlp_kernel,
        grid=(M // tm,),
        in_specs=[pl.BlockSpec((tm, 2 * GM_W), lambda i: (i, 0)),
                  pl.BlockSpec((1, GM_W), lambda i: (0, 0)),
                  pl.BlockSpec((1, GM_W), lambda i: (0, 0)),
                  pl.BlockSpec((GM_HEADS, T, T), lambda i: (0, 0, 0)),
                  pl.BlockSpec((T, GM_W), lambda i: (0, 0))],
        out_specs=pl.BlockSpec((tm, GM_W), lambda i: (i, 0)),
        out_shape=jax.ShapeDtypeStruct((M, GM_W), BF16),
        compiler_params=_params(1),
        name="gmlp",
    )(gm, nw, nb, w_s, bias)


def _compress_kernel(a_ref, pe_ref, w1_ref, w2_ref, o_ref):
    half = w1_ref.shape[0] // 2
    n_seg = a_ref.shape[2]
    a = jnp.concatenate([a_ref[hk, 0] for hk in range(NSA_KV_HEADS)], axis=0)
    u = _dot(a, w1_ref[0:half, :])
    v = _dot(a, w1_ref[half:2 * half, :])
    pe_hi, pe_lo = _split2(pe_ref[...])
    c0 = (_dot(pe_hi, w1_ref[...]) + _dot(pe_lo, w1_ref[...]))[0:1, :]
    h1 = u + pltpu.roll(v, NSA_KV_HEADS * n_seg - 1, 0) + c0
    out = _dot(_gelu(h1).astype(BF16), w2_ref[...])
    valid = (lax.broadcasted_iota(jnp.int32, out.shape, 0) & (n_seg - 1)) < n_seg - 1
    out = jnp.where(valid, out, 0.0).astype(o_ref.dtype)
    for hk in range(NSA_KV_HEADS):
        o_ref[0, :, hk * HEAD_DIM:(hk + 1) * HEAD_DIM] = out[hk * n_seg:(hk + 1) * n_seg]


def _compress(a, pe8, w1, w2):
    _, batch, n_seg, feat = a.shape
    assert n_seg & (n_seg - 1) == 0
    return pl.pallas_call(
        _compress_kernel,
        grid=(batch,),
        in_specs=[pl.BlockSpec((NSA_KV_HEADS, 1, n_seg, feat), lambda i: (0, i, 0, 0)),
                  pl.BlockSpec(pe8.shape, lambda i: (0, 0)),
                  pl.BlockSpec(w1.shape, lambda i: (0, 0)),
                  pl.BlockSpec(w2.shape, lambda i: (0, 0))],
        out_specs=pl.BlockSpec((1, n_seg, KV_W), lambda i: (i, 0, 0)),
        out_shape=jax.ShapeDtypeStruct((batch, n_seg, KV_W), BF16),
        compiler_params=_params(1),
        name="nsa_compress",
    )(a, pe8, w1, w2)


def _stacked_queries(q_ref):
    half = lax.broadcasted_iota(jnp.int32, (q_ref.shape[0], LANES), 1) >> 6
    zero = jnp.zeros((), q_ref.dtype)
    return jnp.concatenate([jnp.where(half == hk, q_ref[:, g * LANES:(g + 1) * LANES], zero)
                            for hk in range(NSA_KV_HEADS) for g in range(NSA_GQA)], axis=0)


def _cmp_select_kernel(q_ref, kc_ref, vc_ref, ov_ref, o_ref, sel_ref, o_s, *, n_blocks):
    Q = Q_BLOCK
    G = NSA_GQA
    HK = NSA_KV_HEADS
    RB = CMP_ROW_BLOCK
    n_cmp = kc_ref.shape[1]
    n_rep = n_cmp // LANES
    q0 = pl.program_id(1) * Q
    qpad = _stacked_queries(q_ref)
    kc = kc_ref[0]
    vc = vc_ref[0]
    t_row = q0 + (lax.broadcasted_iota(jnp.int32, (RB, 1), 0) & (Q - 1))
    valid = lax.broadcasted_iota(jnp.int32, (RB, n_cmp), 1) * CMP_STRIDE + (CMP_LEN - 1) <= t_row
    any_valid = jnp.broadcast_to(t_row >= CMP_LEN - 1, (RB, LANES))
    p_sum = [None] * HK
    for rb in range(HK * G * Q // RB):
        s = _dot_nt(qpad[rb * RB:(rb + 1) * RB], kc)
        sm = jnp.where(valid, s, NEG)
        e = jnp.exp2(sm - _tile_lanes(_row_max(sm), n_rep))
        p = e * _tile_lanes(jnp.where(any_valid, 1.0 / _row_sum(e), 0.0), n_rep)
        o_s[rb * RB:(rb + 1) * RB, :] = _dot(p.astype(BF16), vc)
        for i in range(RB // Q):
            hk = (rb * RB + i * Q) // (G * Q)
            blk = p[i * Q:(i + 1) * Q]
            p_sum[hk] = blk if p_sum[hk] is None else p_sum[hk] + blk
    half = lax.broadcasted_iota(jnp.int32, (Q, LANES), 1) >> 6
    for g in range(G):
        o_ref[:, g * LANES:(g + 1) * LANES] = jnp.where(half == 0, o_s[g * Q:(g + 1) * Q, :],
                                                        o_s[(G + g) * Q:(G + g + 1) * Q, :])

    ov_t = ov_ref[...]
    imp = []
    for hk in range(HK):
        p_hi, p_lo = _split2(p_sum[hk])
        imp.append(_dot_nt(ov_t, p_hi) + _dot_nt(ov_t, p_lo))
    imp = jnp.concatenate(imp, axis=1)

    def choose(rows):
        j = lax.broadcasted_iota(jnp.int32, (rows, HK * Q), 0)
        j_f = j.astype(F32)
        tb = (q0 + (lax.broadcasted_iota(jnp.int32, (rows, HK * Q), 1) & (Q - 1))) >> 6
        forced = (j == 0) | (j == tb) | (j == tb - 1)
        score = jnp.where(j > tb, NEG, jnp.where(forced, BIG, imp[0:rows]))
        score = jnp.where(j < n_blocks, score, -jnp.inf)
        for _ in range(min(SLC_TOPN, n_blocks)):
            top = jnp.max(score, axis=0, keepdims=True)
            first = jnp.min(jnp.where(score == top, j_f, float(LANES)), axis=0, keepdims=True)
            score = jnp.where(j_f == first, -jnp.inf, score)
        neg_sel = jnp.where((score == -jnp.inf) & (j < n_blocks), 0.0, NEG)
        if rows < LANES:
            neg_sel = jnp.concatenate([neg_sel, jnp.full((LANES - rows, HK * Q), NEG, F32)], axis=0)
        for hk in range(HK):
            sel_ref[0, hk] = neg_sel[:, hk * Q:(hk + 1) * Q].T.astype(sel_ref.dtype)

    few = q0 + Q <= (LANES // 2) * SLC_BLOCK

    @pl.when(few)
    def _():
        choose(LANES // 2)

    @pl.when(jnp.logical_not(few))
    def _():
        choose(LANES)


def _cmp_select(q, kc, vc, overlap, batch, seq):
    M = q.shape[0]
    Q = Q_BLOCK
    nq = seq // Q
    n_cmp = kc.shape[1]
    return pl.pallas_call(
        functools.partial(_cmp_select_kernel, n_blocks=seq // SLC_BLOCK),
        grid=(batch, nq),
        in_specs=[pl.BlockSpec((Q, NSA_W), lambda b, i: (b * nq + i, 0)),
                  pl.BlockSpec((1, n_cmp, KV_W), lambda b, i: (b, 0, 0)),
                  pl.BlockSpec((1, n_cmp, KV_W), lambda b, i: (b, 0, 0)),
                  pl.BlockSpec((LANES, n_cmp), lambda b, i: (0, 0))],
        out_specs=[pl.BlockSpec((Q, NSA_W), lambda b, i: (b * nq + i, 0)),
                   pl.BlockSpec((1, NSA_KV_HEADS, Q, LANES), lambda b, i: (b, 0, i, 0))],
        out_shape=[jax.ShapeDtypeStruct((M, NSA_W), F32),
                   jax.ShapeDtypeStruct((batch, NSA_KV_HEADS, seq, LANES), BF16)],
        scratch_shapes=[pltpu.VMEM((NSA_HEADS * Q, LANES), F32)],
        compiler_params=_params(2),
        name="nsa_cmp_select",
    )(q, kc, vc, overlap)


def _window_bias():
    r = np.arange(SLC_Q)[:, None]
    j = np.arange(WINDOW + SLC_Q)[None, :]
    band = (j > r) & (j <= r + WINDOW)
    tabs = [np.where(band & (j >= WINDOW - SLC_Q * v), 0.0, NEG) for v in range(WINDOW // SLC_Q + 1)]
    return np.stack(tabs).astype(np.float32)


def _slc_win_kernel(q_ref, ks_ref, vs_ref, kw_ref, vw_ref, oh_ref, wb_ref, sel_ref, gt_ref, oc_ref, o_ref,
                    m_s, acc_s):
    Q = SLC_Q
    G = NSA_GQA
    HK = NSA_KV_HEADS
    R = HK * G * Q
    RB = SLC_ROW_BLOCK
    KC = SLC_KEYS
    NW = WINDOW // Q + 1
    q0 = pl.program_id(1) * Q
    n_full = q0 // KC
    t_row = q0 + (lax.broadcasted_iota(jnp.int32, (RB, 1), 0) & (Q - 1))
    key_off = lax.broadcasted_iota(jnp.int32, (RB, KC), 1)
    half_q = lax.broadcasted_iota(jnp.int32, (Q, LANES), 1) >> 6
    qpad = _stacked_queries(q_ref)
    qaug =jnp.concatenate([qpad, jnp.concatenate([sel_ref[0, hk] for hk in range(HK) for _ in range(G)], axis=0)],
                           axis=1)
    m_s[...] = jnp.full(m_s.shape, NEG, F32)
    acc_s[...] = jnp.zeros(acc_s.shape, F32)

    def slc_chunk(c, causal):
        k0 = pl.multiple_of(c * KC, KC)
        kaug = jnp.concatenate([ks_ref[pl.ds(k0, KC), :], oh_ref[pl.ds(k0, KC), :]], axis=1)
        vaug = jnp.concatenate([vs_ref[pl.ds(k0, KC), :], jnp.ones((KC, LANES), BF16)], axis=1)
        for rb in range(R // RB):
            rows = slice(rb * RB, (rb + 1) * RB)
            s = _dot_nt(qaug[rows], kaug)
            if causal:
                s = jnp.where(k0 + key_off <= t_row, s, NEG)
            m_old = m_s[rows, :]
            m_new = jnp.maximum(m_old, jnp.max(s, axis=-1, keepdims=True))
            p = jnp.exp2(s - _tile_lanes(m_new, KC // LANES)).astype(BF16)
            acc_s[rows, :] = _tile_lanes(jnp.exp2(m_old - m_new), 2) * acc_s[rows, :] + _dot(p, vaug)
            m_s[rows, :] = m_new

    def run(c0, n):
        for i in range(n):
            slc_chunk(c0 + i, False)

    def body(i, carry):
        run(i * SLC_GROUP, SLC_GROUP)
        return carry

    lax.fori_loop(0, n_full // SLC_GROUP, body, 0)
    base = n_full - n_full % SLC_GROUP
    size = SLC_GROUP // 2
    while size >= 1:
        rest = n_full % (2 * size)

        @pl.when(rest >= size)
        def _(base=base, size=size):
            run(base, size)

        base = base + jnp.where(rest >= size, size, 0)
        size //= 2
    slc_chunk(n_full, True)
    acc = acc_s[...]
    o_slc = acc[:, 0:LANES] / acc[:, LANES:2 * LANES]

    rows_k, rows_v = [], []
    for w in range(NW):
        k0 = pl.multiple_of(jnp.maximum(q0 - WINDOW + w * Q, 0), Q)
        rows_k.append(kw_ref[pl.ds(k0, Q), :])
        rows_v.append(vw_ref[pl.ds(k0, Q), :])
    k_win = jnp.concatenate(rows_k, axis=0)
    v_win = jnp.concatenate([jnp.concatenate(rows_v, axis=0), jnp.ones((NW * Q, LANES), BF16)], axis=1)
    s = _dot_nt(qpad, k_win) + jnp.concatenate([wb_ref[0]] * (HK * G), axis=0)
    p = jnp.exp2(s - _tile_lanes(_row_max(s), NW * Q // LANES)).astype(BF16)
    acc = _dot(p, v_win)
    o_win = acc[:, 0:LANES] / acc[:, LANES:2 * LANES]

    gates = gt_ref[...]
    for g in range(G):
        oc = oc_ref[:, g * LANES:(g + 1) * LANES]
        tiles = []
        for hk in range(HK):
            c0 = 3 * (hk * G + g)
            rows = slice((hk * G + g) * Q, (hk * G + g + 1) * Q)
            tiles.append(gates[:, c0:c0 + 1] * oc + gates[:, c0 + 1:c0 + 2] * o_slc[rows]
                         + gates[:, c0 + 2:c0 + 3] * o_win[rows])
        o_ref[:, g * LANES:(g + 1) * LANES] = jnp.where(half_q == 0, tiles[0], tiles[1]).astype(o_ref.dtype)


def _slc_win(q, k, v, onehot, wbias, sel, gates, o_cmp, batch, seq):
    M = q.shape[0]
    Q = SLC_Q
    G = NSA_GQA
    nq = seq // Q
    n_var = wbias.shape[0]
    qmap = lambda b, i: (b * nq + i, 0)
    return pl.pallas_call(
        _slc_win_kernel,
        grid=(batch, nq),
        in_specs=[pl.BlockSpec((Q, NSA_W), qmap),
                  pl.BlockSpec((seq, KV_W), lambda b, i: (b, 0)),
                  pl.BlockSpec((seq, KV_W), lambda b, i: (b, 0)),
                  pl.BlockSpec((seq, KV_W), lambda b, i: (b, 1)),
                  pl.BlockSpec((seq, KV_W), lambda b, i: (b, 1)),
                  pl.BlockSpec((seq, LANES), lambda b, i: (0, 0)),
                  pl.BlockSpec((1,) + wbias.shape[1:], lambda b, i: (jnp.minimum(i, n_var - 1), 0, 0)),
                  pl.BlockSpec((1, NSA_KV_HEADS, Q, LANES), lambda b, i: (b, 0, i, 0)),
                  pl.BlockSpec((Q, LANES), qmap),
                  pl.BlockSpec((Q, NSA_W), qmap)],
        out_specs=pl.BlockSpec((Q, NSA_W), qmap),
        out_shape=jax.ShapeDtypeStruct((M, NSA_W), BF16),
        scratch_shapes=[pltpu.VMEM((NSA_KV_HEADS * G * Q, LANES), F32),
                        pltpu.VMEM((NSA_KV_HEADS * G * Q, 2 * LANES), F32)],
        compiler_params=_params(2),
        name="nsa_slc_win",
    )(q, k, v, k, v, onehot, wbias, sel, gates, o_cmp)


def _outproj_kernel(oh_ref, on_ref, og_ref, w_ref, x_ref, g_ref, lw_ref, lb_ref, o_ref, *, alpha):
    mix = (_dot(oh_ref[...], w_ref[0:HG_W, :])
           + _dot(on_ref[...], w_ref[HG_W:HG_W + NSA_W, :])
           + _dot(og_ref[...], w_ref[HG_W + NSA_W:HG_W + NSA_W + GM_W, :]))
    y = alpha * x_ref[...] + (1.0 + g_ref[0]) * mix
    o_ref[...] = _layer_norm(y, lw_ref[...], lb_ref[...])


def _out_projection(o_h, o_n, o_g, w_o, x2, g1, ln_w, ln_b, seq, alpha):
    M, D = x2.shape
    tm = ROW_TILE
    per_b = seq // tm
    row = lambda i: (i, 0)
    const = lambda i: (0, 0)
    return pl.pallas_call(
        functools.partial(_outproj_kernel, alpha=alpha),
        grid=(M // tm,),
        in_specs=[pl.BlockSpec((tm, HG_W), row),
                  pl.BlockSpec((tm, NSA_W), row),
                  pl.BlockSpec((tm, GM_W), row),
                  pl.BlockSpec(w_o.shape, const),
                  pl.BlockSpec((tm, D), row),
                  pl.BlockSpec((1, 1, D), lambda i: (i // per_b, 0, 0)),
                  pl.BlockSpec((1, D), const),
                  pl.BlockSpec((1, D), const)],
        out_specs=pl.BlockSpec((tm, D), row),
        out_shape=jax.ShapeDtypeStruct((M, D), F32),
        compiler_params=_params(1),
        name="out_proj_ln",
    )(o_h, o_n, o_g, w_o, x2, g1, ln_w, ln_b)


def _ffn_kernel(x_ref, sc_ref, sh_ref, g_ref, w1_ref, w2_ref, lw_ref, lb_ref, o_ref, *, alpha, ff_chunk):
    x = x_ref[...]
    h = (x * (1.0 + sc_ref[0]) + sh_ref[0]).astype(BF16)
    y = jnp.zeros(x.shape, F32)
    for c in range(w1_ref.shape[1] // ff_chunk):
        a = jnp.maximum(_dot(h, w1_ref[:, c * ff_chunk:(c + 1) * ff_chunk]), 0.0)
        y = y + _dot((a * a).astype(BF16), w2_ref[c * ff_chunk:(c + 1) * ff_chunk, :])
    z = alpha * x + (1.0 + g_ref[0]) * y
    o_ref[...] = _layer_norm(z, lw_ref[...], lb_ref[...])


def _ffn(x2, sc, sh, g2, w1, w2, ln_w, ln_b, seq, alpha):
    M, D = x2.shape
    tm = ROW_TILE
    per_b = seq // tm
    row = lambda i: (i, 0)
    const = lambda i: (0, 0)
    bmap = lambda i: (i // per_b, 0, 0)
    return pl.pallas_call(
        functools.partial(_ffn_kernel, alpha=alpha, ff_chunk=1024),
        grid=(M // tm,),
        in_specs=[pl.BlockSpec((tm, D), row),
                  pl.BlockSpec((1, 1, D), bmap),
                  pl.BlockSpec((1, 1, D), bmap),
                  pl.BlockSpec((1, 1, D), bmap),
                  pl.BlockSpec(w1.shape, const),
                  pl.BlockSpec(w2.shape, const),
                  pl.BlockSpec((1, D), const),
                  pl.BlockSpec((1, D), const)],
        out_specs=pl.BlockSpec((tm, D), row),
        out_shape=jax.ShapeDtypeStruct((M, D), F32),
        compiler_params=_params(1),
        name="ffn_ln",
    )(x2, sc, sh, g2, w1, w2, ln_w, ln_b)


def kernel(x, c, positions, w_in, w_o, hgrn_lower_bounds, hgrn_norm_w, cmp_pe_k, cmp_w1_k, cmp_w2_k, cmp_pe_v, cmp_w1_v, cmp_w2_v, gmlp_norm_w, gmlp_norm_b, gmlp_w_s, gmlp_b_s, w_ff1, w_ff2, w_ada, b_ada, ln1_w, ln1_b, ln2_w, ln2_b):
    B, S, D = x.shape
    L = w_in.shape[0]
    M = B * S
    alpha = (2 * L) ** 0.25
    n_seg = S // CMP_STRIDE

    inv_freq = ROPE_THETA ** (-jnp.arange(0, HEAD_DIM, 2, dtype=F32) / HEAD_DIM)
    ang = positions.astype(F32)[..., None] * inv_freq
    cos, sin = jnp.cos(ang).reshape(M, HEAD_DIM // 2), jnp.sin(ang).reshape(M, HEAD_DIM // 2)
    cos, sin = lax.optimization_barrier((cos, sin))
    cos_f = jnp.concatenate([cos, cos, cos, cos], axis=-1)
    sin_s = jnp.concatenate([-sin, sin, -sin, sin], axis=-1)

    lb_sm = jax.nn.softmax(hgrn_lower_bounds.astype(F32), axis=0)
    lb_all = jnp.cumsum(lb_sm, axis=0) - lb_sm[0:1]

    col_idx, n_cols = _in_col_index()
    assert n_cols == w_in.shape[2]
    step = np.where(col_idx >= 0, 1, 0)
    cuts = [0] + [i for i in range(1, len(col_idx)) if col_idx[i] != col_idx[i - 1] + step[i]] + [len(col_idx)]
    pieces = [w_in[:, :, col_idx[a]:col_idx[b - 1] + 1].astype(BF16) if col_idx[a] >= 0
              else jnp.zeros(w_in.shape[:2] + (b - a,), BF16) for a, b in zip(cuts[:-1], cuts[1:])]
    w_cat = jnp.concatenate(pieces, axis=2)

    lane_head = np.arange(HG_W) // HEAD_DIM
    bd = jnp.asarray((lane_head[:, None] == lane_head[None, :]).astype(np.float32), dtype=BF16)
    hg_window, hg_pair_mask = _hgrn_tables()
    hg_window = jnp.asarray(hg_window, dtype=BF16)
    hg_pair_mask = jnp.asarray(hg_pair_mask, dtype=F32)
    cs = np.arange(n_seg) * CMP_STRIDE
    ss = np.arange(LANES) * SLC_BLOCK
    ov = (cs[:, None] < ss[None, :] + SLC_BLOCK) & (ss[None, :] < cs[:, None] + CMP_LEN)
    ov &= (np.arange(n_seg) < n_seg - 1)[:, None] & (np.arange(LANES) < S // SLC_BLOCK)[None, :]
    overlap = jnp.asarray(ov.T.astype(np.float32), dtype=BF16)
    assert S % SLC_KEYS == 0 and S // SLC_BLOCK <= LANES
    onehot = (jnp.arange(S)[:, None] // SLC_BLOCK == jnp.arange(LANES)[None, :]).astype(BF16)
    wbias = jnp.asarray(_window_bias())
    head_rows = [w_o[:, HG_W + h * HEAD_DIM:HG_W + (h + 1) * HEAD_DIM]
                 for h in _nsa_head_perm()[::HEAD_DIM] // HEAD_DIM]
    w_o = jnp.concatenate([w_o[:, :HG_W]] + head_rows + [w_o[:, HG_W + NSA_W:]], axis=1).astype(BF16)

    c8 = jnp.pad(c, ((0, 8 - B), (0, 0)))
    mod = _modulation(c8, w_ada, b_ada)[:, :B]

    x2 = x.reshape(M, D)
    for l in range(L):
        sh1, sc1, g1, sh2, sc2, g2 = [mod[l, :, i * D:(i + 1) * D].reshape(B, 1, D) for i in range(6)]
        hg, q, k, v, k_cmp, v_cmp, gm, gates = _in_projection(x2, sc1, sh1, w_cat[l], cos_f, sin_s, S)

        o_h = _hgrn2(hg, lb_all[l].reshape(1, HG_W), jnp.tile(hgrn_norm_w[l], HG_HEADS).reshape(1, HG_W), bd,
                     hg_window, hg_pair_mask, S)

        bias = jnp.repeat(gmlp_b_s[l].T, HEAD_DIM, axis=1)
        o_g = _gmlp(gm, gmlp_norm_w[l].reshape(1, GM_W), gmlp_norm_b[l].reshape(1, GM_W), gmlp_w_s[l], bias)

        pe_k = jnp.pad(cmp_pe_k[l].reshape(1, -1), ((0, 7), (0, 0)))
        pe_v = jnp.pad(cmp_pe_v[l].reshape(1, -1), ((0, 7), (0, 0)))
        seg_shape = (NSA_KV_HEADS, B, n_seg, CMP_STRIDE * HEAD_DIM)
        kc = _compress(k_cmp.reshape(seg_shape), pe_k, cmp_w1_k[l].astype(BF16), cmp_w2_k[l].astype(BF16))
        vc = _compress(v_cmp.reshape(seg_shape), pe_v, cmp_w1_v[l].astype(BF16), cmp_w2_v[l].astype(BF16))
        o_cmp, sel = _cmp_select(q, kc, vc, overlap, B, S)
        o_n = _slc_win(q, k, v, onehot, wbias, sel, gates, o_cmp, B, S)

        x2 = _out_projection(o_h, o_n, o_g, w_o[l].astype(BF16), x2, g1,
                             ln1_w[l].reshape(1, D), ln1_b[l].reshape(1, D), S, alpha)
        x2 = _ffn(x2, sc2, sh2, g2, w_ff1[l].astype(BF16), w_ff2[l].astype(BF16),
                  ln2_w[l].reshape(1, D), ln2_b[l].reshape(1, D), S, alpha)
    return x2.reshape(B, S, D)
```

```python
import functools

import numpy as np
import jax
import jax.numpy as jnp
from jax import lax
from jax.experimental import pallas as pl
from jax.experimental.pallas import tpu as pltpu

F32 = jnp.float32
BF16 = jnp.bfloat16

HEAD_DIM = 64
HG_HEADS = 4
HG_W = HG_HEADS * HEAD_DIM
NSA_HEADS = 8
NSA_W = NSA_HEADS * HEAD_DIM
NSA_GQA = 4
NSA_KV_HEADS = NSA_HEADS // NSA_GQA
KV_W = NSA_KV_HEADS * HEAD_DIM
GM_HEADS = 4
GM_W = GM_HEADS * HEAD_DIM
N_GATES = 3 * NSA_HEADS
ROPE_THETA = 10000.0
HGRN_CHUNK = 64
CMP_LEN = 32
CMP_STRIDE = 16
SLC_BLOCK = 64
SLC_TOPN = 16
WINDOW = 512
Q_BLOCK = 256
GMLP_CHUNK = 128
NEG = -1e30
BIG = 1e30
F_MIN = 1e-30
LANES = 128
HEAD_SHIFT = HEAD_DIM.bit_length() - 1
BLOCK_SHIFT = SLC_BLOCK.bit_length() - 1
ROW_TILE = 512
SLC_Q = 256
SLC_KEYS = 512
SLC_GROUP = 8
SLC_ROW_BLOCK = 512
CMP_ROW_BLOCK = 1024
Q_SCALE = HEAD_DIM ** -0.5 * 1.4426950408889634
VMEM_LIMIT = 56 * 1024 * 1024


def _params(n_axes, semantics="parallel"):
    return pltpu.CompilerParams(dimension_semantics=(semantics,) * n_axes, vmem_limit_bytes=VMEM_LIMIT)


def _dot(a, b):
    return jnp.dot(a, b, preferred_element_type=F32)


def _dot_nt(a, b):
    return lax.dot_general(a, b, (((1,), (1,)), ((), ())), preferred_element_type=F32)


def _split2(x):
    hi = x.astype(BF16)
    lo = (x - hi.astype(F32)).astype(BF16)
    return hi, lo


def _split3(x):
    hi = x.astype(BF16)
    r = x - hi.astype(F32)
    mid = r.astype(BF16)
    lo = (r - mid.astype(F32)).astype(BF16)
    return hi, mid, lo


def _tile_lanes(x, n):
    return x if n == 1 else jnp.concatenate([x] * n, axis=1)


def _row_max(s):
    return jnp.broadcast_to(jnp.max(s, axis=-1, keepdims=True), (s.shape[0], LANES))


def _row_sum(s):
    return jnp.broadcast_to(jnp.sum(s, axis=-1, keepdims=True), (s.shape[0], LANES))


def _layer_norm(y, w, b):
    mu = jnp.mean(y, axis=-1, keepdims=True)
    d = y - mu
    var = jnp.mean(d * d, axis=-1, keepdims=True)
    return d * lax.rsqrt(var + 1e-5) * w + b


def _mod_kernel(c_ref, w_ref, b_ref, o_ref):
    a = c_ref[...]
    a = a * jax.nn.sigmoid(a)
    ah, al = _split2(a)
    wh, wl = _split2(w_ref[0])
    o_ref[0] = _dot(ah, wh) + _dot(al, wh) + _dot(ah, wl) + b_ref[0]


def _modulation(c8, w_ada, b_ada):
    L, D, N = w_ada.shape
    tn = 1024
    return pl.pallas_call(
        _mod_kernel,
        grid=(L, N // tn),
        in_specs=[pl.BlockSpec((8, D), lambda l, j: (0, 0)),
                  pl.BlockSpec((1, D, tn), lambda l, j: (l, 0, j)),
                  pl.BlockSpec((1, 1, tn), lambda l, j: (l, 0, j))],
        out_specs=pl.BlockSpec((1, 8, tn), lambda l, j: (l, 0, j)),
        out_shape=jax.ShapeDtypeStruct((L, 8, N), F32),
        compiler_params=_params(2),
        name="adaln_mod",
    )(c8, w_ada, b_ada.reshape(L, 1, N))


C_HG = 0
C_Q = C_HG + 4 * HG_W
C_K = C_Q + NSA_W
C_V = C_K + 3 * KV_W
C_GM = C_V + 3 * KV_W
C_GT = C_GM + 2 * GM_W
C_END = C_GT + LANES


def _nsa_head_perm():
    heads = [hk * NSA_GQA + g for g in range(NSA_GQA) for hk in range(NSA_KV_HEADS)]
    return np.concatenate([np.arange(h * HEAD_DIM, (h + 1) * HEAD_DIM) for h in heads])


def _in_col_index():
    o = {}
    off = 0
    for name, size in (("hq", HG_W), ("hf", HG_W), ("hi", HG_W), ("hg", HG_W), ("nq", NSA_W),
                       ("kcm", KV_W), ("vcm", KV_W), ("ksl", KV_W), ("vsl", KV_W), ("kwn", KV_W),
                       ("vwn", KV_W), ("ngt", N_GATES), ("gu", GM_W), ("gv", GM_W)):
        o[name] = np.arange(off, off + size)
        off += size
    idx = np.concatenate([o["hq"], o["hf"], o["hi"], o["hg"], o["nq"][_nsa_head_perm()],
                          o["kcm"], o["ksl"], o["kwn"], o["vcm"], o["vsl"], o["vwn"],
                          o["gu"], o["gv"], o["ngt"], np.full(LANES - N_GATES, -1)])
    return idx, off


def _rope(a, cos_f, sin_s):
    lane = lax.broadcasted_iota(jnp.int32, a.shape, 1)
    first = (lane & (HEAD_DIM - 1)) < HEAD_DIM // 2
    partner = jnp.where(first, pltpu.roll(a, LANES - HEAD_DIM // 2, 1), pltpu.roll(a, HEAD_DIM // 2, 1))
    return a * cos_f + partner * sin_s


def _inproj_kernel(x_ref, sc_ref, sh_ref, w_ref, cos_ref, sin_ref,
                   hg_ref, q_ref, k_ref, v_ref, kc_ref, vc_ref, gm_ref, gt_ref, seg_s):
    h = (x_ref[...] * (1.0 + sc_ref[0]) + sh_ref[0]).astype(BF16)
    cos_f = cos_ref[...]
    sin_s = sin_ref[...]
    y = _dot(h, w_ref[...])
    tile = lambda c: y[:, c:c + LANES]
    hg_ref[...] = y[:, C_HG:C_Q]
    for j in range(NSA_W // LANES):
        q_ref[:, j * LANES:(j + 1) * LANES] = (_rope(tile(C_Q + j * LANES), cos_f, sin_s) * Q_SCALE).astype(BF16)
    seg_s[0] = _rope(tile(C_K), cos_f, sin_s)
    seg_s[1] = tile(C_V)
    n_seg = x_ref.shape[0] // CMP_STRIDE
    for p in range(CMP_STRIDE):
        for src, dst in ((0, kc_ref), (1, vc_ref)):
            rows = seg_s[src, pl.ds(p, n_seg, stride=CMP_STRIDE), :].astype(BF16)
            for hk in range(NSA_KV_HEADS):
                dst[hk, :, p * HEAD_DIM:(p + 1) * HEAD_DIM] = rows[:, hk * HEAD_DIM:(hk + 1) * HEAD_DIM]
    for j in range(2):
        k_ref[:, j * LANES:(j + 1) * LANES] = _rope(tile(C_K + (j + 1) * LANES), cos_f, sin_s).astype(BF16)
    v_ref[...] = y[:, C_V + LANES:C_GM].astype(BF16)
    gm_ref[...] = y[:, C_GM:C_GT]
    gt_ref[...] = jax.nn.sigmoid(tile(C_GT))


def _in_projection(x2, sc, sh, w_cat, cos_f, sin_s, seq):
    M, D = x2.shape
    tm = ROW_TILE
    per_b = seq // tm
    seg_w = CMP_STRIDE * HEAD_DIM
    row = lambda i: (i, 0)
    bmap = lambda i: (i // per_b, 0, 0)
    return pl.pallas_call(
        _inproj_kernel,
        grid=(M // tm,),
        in_specs=[pl.BlockSpec((tm, D), row),
                  pl.BlockSpec((1, 1, D), bmap),
                  pl.BlockSpec((1, 1, D), bmap),
                  pl.BlockSpec((D, C_END), lambda i: (0, 0)),
                  pl.BlockSpec((tm, LANES), row),
                  pl.BlockSpec((tm, LANES), row)],
        out_specs=[pl.BlockSpec((tm, 4 * HG_W), row),
                   pl.BlockSpec((tm, NSA_W), row),
                   pl.BlockSpec((tm, 2 * KV_W), row),
                   pl.BlockSpec((tm, 2 * KV_W), row),
                   pl.BlockSpec((NSA_KV_HEADS, tm // CMP_STRIDE, seg_w), lambda i: (0, i, 0)),
                   pl.BlockSpec((NSA_KV_HEADS, tm // CMP_STRIDE, seg_w), lambda i: (0, i, 0)),
                   pl.BlockSpec((tm, 2 * GM_W), row),
                   pl.BlockSpec((tm, LANES), row)],
        out_shape=[jax.ShapeDtypeStruct((M, 4 * HG_W), F32),
                   jax.ShapeDtypeStruct((M, NSA_W), BF16),
                   jax.ShapeDtypeStruct((M, 2 * KV_W), BF16),
                   jax.ShapeDtypeStruct((M, 2 * KV_W), BF16),
                   jax.ShapeDtypeStruct((NSA_KV_HEADS, M // CMP_STRIDE, seg_w), BF16),
                   jax.ShapeDtypeStruct((NSA_KV_HEADS, M // CMP_STRIDE, seg_w), BF16),
                   jax.ShapeDtypeStruct((M, 2 * GM_W), F32),
                   jax.ShapeDtypeStruct((M, LANES), F32)],
        scratch_shapes=[pltpu.VMEM((2, tm, LANES), F32)],
        compiler_params=_params(1),
        name="in_proj",
    )(x2, sc, sh, w_cat, cos_f, sin_s)


HGRN_LEVELS = 6
HGRN_GROUP = 2


def _hgrn_tables():
    C = HGRN_CHUNK
    r = np.arange(C)
    wins = [r[None, :] <= r[:, None], r[None, :] > r[:, None]]
    masks = []
    for lvl in range(HGRN_LEVELS):
        c = C >> (lvl + 1)
        blk = r // c
        pair = blk // 2
        upper = blk % 2 == 1
        bound = (2 * pair + 1) * c - 1
        win_u = (r[None, :] > bound[:, None]) & (r[None, :] <= r[:, None])
        win_l = (r[None, :] > r[:, None]) & (r[None, :] <= bound[:, None])
        wins.append(np.where(upper[:, None], win_u, win_l))
        masks.append(upper[:, None] & ~upper[None, :] & (pair[:, None] == pair[None, :]))
    masks.append(np.eye(C, dtype=bool))
    window = np.concatenate(wins, axis=0).astype(np.float32)
    pair_mask = np.stack([np.tile(m, (1, HG_HEADS)) for m in masks]).astype(np.float32)
    return window, pair_mask


def _hgrn_kernel(x_ref, lb_ref, nw_ref, bd_ref, win_ref, pm_ref, o_ref, st_ref, *, steps_per_seq):
    C = HGRN_CHUNK
    W = HG_W

    @pl.when(pl.program_id(0) % steps_per_seq == 0)
    def _():
        st_ref[...] = jnp.zeros_like(st_ref)

    lb = lb_ref[...]
    nw = nw_ref[...]
    bd = bd_ref[...]
    bd_f = bd.astype(F32)
    lane_head = lax.broadcasted_iota(jnp.int32, (C, W), 1) >> HEAD_SHIFT
    own = [lane_head == h for h in range(HG_HEADS)]

    def stack(a):
        return jnp.concatenate([jnp.where(own[h], a, 0.0) for h in range(HG_HEADS)], axis=0).astype(BF16)

    def intra(r0):
        q = x_ref[pl.ds(r0, C), 0:W]
        z = x_ref[pl.ds(r0, C), W:2 * W]
        v = x_ref[pl.ds(r0, C), 2 * W:3 * W]
        f = lb + (1.0 - lb) * jax.nn.sigmoid(z)
        log_f = jnp.log(jnp.maximum(f, F_MIN))
        kk = (1.0 - lb) * jax.nn.sigmoid(-z)
        qf = q * jax.nn.sigmoid(q)
        l_hi, l_lo = _split2(log_f)
        xw = _dot(win_ref[...], jnp.concatenate([l_hi, l_lo], axis=1))
        e = jnp.exp(xw[:, 0:W] + xw[:, W:2 * W])
        e_b = e[0:C]
        attn = _dot_nt(qf.astype(BF16), stack(kk)) * pm_ref[HGRN_LEVELS]
        for lvl in range(HGRN_LEVELS):
            e_l = e[(2 + lvl) * C:(3 + lvl) * C]
            attn = attn + _dot_nt((qf * e_l).astype(BF16), stack(kk * e_l)) * pm_ref[lvl]
        o_intra = _dot(attn.astype(BF16), stack(v))
        upd = lax.dot_general(v.astype(BF16), (kk * e[C:2 * C]).astype(BF16), (((0,), (0,)), ((), ())),
                              preferred_element_type=F32) * bd_f
        return o_intra, (qf * e_b).astype(BF16), upd, e_b[C - 1:C, :]

    def finish(r0, o_intra, q_decayed, upd, e_last):
        st = st_ref[...]
        o = o_intra + _dot_nt(q_decayed, st.astype(BF16))
        st_ref[...] = st * e_last + upd
        oo = o * o
        o_hi, o_lo = _split2(oo)
        ms = (_dot(o_hi, bd) + _dot(o_lo, bd)) * (1.0 / HEAD_DIM)
        y = o * lax.rsqrt(ms + 1e-6) * nw
        g = x_ref[pl.ds(r0, C), 3 * W:4 * W]
        o_ref[pl.ds(r0, C), :] = (y * (g * jax.nn.sigmoid(g))).astype(o_ref.dtype)

    def chunk_group(ci, carry):
        rows = [pl.multiple_of(ci * (HGRN_GROUP * C) + i * C, C) for i in range(HGRN_GROUP)]
        parts = [intra(r) for r in rows]
        for r, part in zip(rows, parts):
            finish(r, *part)
        return carry

    lax.fori_loop(0, x_ref.shape[0] // (HGRN_GROUP * C), chunk_group, 0)


def _hgrn2(hg, lb, nw, bd, window, pair_mask, seq):
    M = hg.shape[0]
    tm = ROW_TILE
    return pl.pallas_call(
        functools.partial(_hgrn_kernel, steps_per_seq=seq // tm),
        grid=(M // tm,),
        in_specs=[pl.BlockSpec((tm, 4 * HG_W), lambda i: (i, 0)),
                  pl.BlockSpec((1, HG_W), lambda i: (0, 0)),
                  pl.BlockSpec((1, HG_W), lambda i: (0, 0)),
                  pl.BlockSpec((HG_W, HG_W), lambda i: (0, 0)),
                  pl.BlockSpec(window.shape, lambda i: (0, 0)),
                  pl.BlockSpec(pair_mask.shape, lambda i: (0, 0, 0))],
        out_specs=pl.BlockSpec((tm, HG_W), lambda i: (i, 0)),
        out_shape=jax.ShapeDtypeStruct((M, HG_W), BF16),
        scratch_shapes=[pltpu.VMEM((HG_W, HG_W), F32)],
        compiler_params=_params(1, "arbitrary"),
        name="hgrn2",
    )(hg, lb, nw, bd, window, pair_mask)


def _gelu(x):
    return jax.nn.gelu(x)


def _gmlp_kernel(x_ref, nw_ref, nb_ref, ws_ref, bias_ref, o_ref):
    T = GMLP_CHUNK
    u = _gelu(x_ref[:, 0:GM_W])
    v = _layer_norm(_gelu(x_ref[:, GM_W:2 * GM_W]), nw_ref[...], nb_ref[...]).astype(BF16)
    lower = lax.broadcasted_iota(jnp.int32, (T, T), 0) >= lax.broadcasted_iota(jnp.int32, (T, T), 1)
    head = lax.broadcasted_iota(jnp.int32, (T, GM_W), 1) >> HEAD_SHIFT
    bias = bias_ref[...]
    for c in range(x_ref.shape[0] // T):
        vc = v[c * T:(c + 1) * T]
        sv = bias
        for g in range(GM_HEADS):
            w = jnp.where(lower, ws_ref[g], 0.0).astype(BF16)
            sv = sv + jnp.where(head == g, _dot(w, vc), 0.0)
        o_ref[c * T:(c + 1) * T, :] = (u[c * T:(c + 1) * T] * sv).astype(o_ref.dtype)


def _gmlp(gm, nw, nb, w_s, bias):
    M = gm.shape[0]
    tm = ROW_TILE
    T = GMLP_CHUNK
    return pl.pallas_call(
        _gmlp_kernel,
        grid=(M // tm,),
        in_specs=[pl.BlockSpec((tm, 2 * GM_W), lambda i: (i, 0)),
                  pl.BlockSpec((1, GM_W), lambda i: (0, 0)),
                  pl.BlockSpec((1, GM_W), lambda i: (0, 0)),
                  pl.BlockSpec((GM_HEADS, T, T), lambda i: (0, 0, 0)),
                  pl.BlockSpec((T, GM_W), lambda i: (0, 0))],
        out_specs=pl.BlockSpec((tm, GM_W), lambda i: (i, 0)),
        out_shape=jax.ShapeDtypeStruct((M, GM_W), BF16),
        compiler_params=_params(1),
        name="gmlp",
    )(gm, nw, nb, w_s, bias)


def _compress_kernel(a_ref, pe_ref, w1_ref, w2_ref, o_ref):
    half = w1_ref.shape[0] // 2
    n_seg = a_ref.shape[2]
    a = jnp.concatenate([a_ref[hk, 0] for hk in range(NSA_KV_HEADS)], axis=0)
    u = _dot(a, w1_ref[0:half, :])
    v = _dot(a, w1_ref[half:2 * half, :])
    pe_hi, pe_lo = _split2(pe_ref[...])
    c0 = (_dot(pe_hi, w1_ref[...]) + _dot(pe_lo, w1_ref[...]))[0:1, :]
    h1 = u + pltpu.roll(v, NSA_KV_HEADS * n_seg - 1, 0) + c0
    out = _dot(_gelu(h1).astype(BF16), w2_ref[...])
    valid = (lax.broadcasted_iota(jnp.int32, out.shape, 0) & (n_seg - 1)) < n_seg - 1
    out = jnp.where(valid, out, 0.0).astype(o_ref.dtype)
    for hk in range(NSA_KV_HEADS):
        o_ref[0, :, hk * HEAD_DIM:(hk + 1) * HEAD_DIM] = out[hk * n_seg:(hk + 1) * n_seg]


def _compress(a, pe8, w1, w2):
    _, batch, n_seg, feat = a.shape
    assert n_seg & (n_seg - 1) == 0
    return pl.pallas_call(
        _compress_kernel,
        grid=(batch,),
        in_specs=[pl.BlockSpec((NSA_KV_HEADS, 1, n_seg, feat), lambda i: (0, i, 0, 0)),
                  pl.BlockSpec(pe8.shape, lambda i: (0, 0)),
                  pl.BlockSpec(w1.shape, lambda i: (0, 0)),
                  pl.BlockSpec(w2.shape, lambda i: (0, 0))],
        out_specs=pl.BlockSpec((1, n_seg, KV_W), lambda i: (i, 0, 0)),
        out_shape=jax.ShapeDtypeStruct((batch, n_seg, KV_W), BF16),
        compiler_params=_params(1),
        name="nsa_compress",
    )(a, pe8, w1, w2)


def _stacked_queries(q_ref):
    half = lax.broadcasted_iota(jnp.int32, (q_ref.shape[0], LANES), 1) >> HEAD_SHIFT
    zero = jnp.zeros((), q_ref.dtype)
    return jnp.concatenate([jnp.where(half == hk, q_ref[:, g * LANES:(g + 1) * LANES], zero)
                            for hk in range(NSA_KV_HEADS) for g in range(NSA_GQA)], axis=0)


def _cmp_select_kernel(q_ref, kc_ref, vc_ref, ov_ref, o_ref, sel_ref, o_s, *, n_blocks):
    Q = Q_BLOCK
    G = NSA_GQA
    HK = NSA_KV_HEADS
    RB = CMP_ROW_BLOCK
    n_cmp = kc_ref.shape[1]
    n_rep = n_cmp // LANES
    q0 = pl.program_id(1) * Q
    qpad = _stacked_queries(q_ref)
    kc = kc_ref[0]
    vc = vc_ref[0]
    t_row = q0 + (lax.broadcasted_iota(jnp.int32, (RB, 1), 0) & (Q - 1))
    valid = lax.broadcasted_iota(jnp.int32, (RB, n_cmp), 1) * CMP_STRIDE + (CMP_LEN - 1) <= t_row
    any_valid = jnp.broadcast_to(t_row >= CMP_LEN - 1, (RB, LANES))
    p_sum = [None] * HK
    for rb in range(HK * G * Q // RB):
        s = _dot_nt(qpad[rb * RB:(rb + 1) * RB], kc)
        sm = jnp.where(valid, s, NEG)
        e = jnp.exp2(sm - _tile_lanes(_row_max(sm), n_rep))
        p = e * _tile_lanes(jnp.where(any_valid, 1.0 / _row_sum(e), 0.0), n_rep)
        o_s[rb * RB:(rb + 1) * RB, :] = _dot(p.astype(BF16), vc)
        for i in range(RB // Q):
            hk = (rb * RB + i * Q) // (G * Q)
            blk = p[i * Q:(i + 1) * Q]
            p_sum[hk] = blk if p_sum[hk] is None else p_sum[hk] + blk
    half = lax.broadcasted_iota(jnp.int32, (Q, LANES), 1) >> HEAD_SHIFT
    for g in range(G):
        o_ref[:, g * LANES:(g + 1) * LANES] = jnp.where(half == 0, o_s[g * Q:(g + 1) * Q, :],
                                                        o_s[(G + g) * Q:(G + g + 1) * Q, :])

    ov_t = ov_ref[...]
    imp = []
    for hk in range(HK):
        p_hi, p_lo = _split2(p_sum[hk])
        imp.append(_dot_nt(ov_t, p_hi) + _dot_nt(ov_t, p_lo))
    imp = jnp.concatenate(imp, axis=1)

    def choose(rows):
        j = lax.broadcasted_iota(jnp.int32, (rows, HK * Q), 0)
        j_f = j.astype(F32)
        tb = (q0 + (lax.broadcasted_iota(jnp.int32, (rows, HK * Q), 1) & (Q - 1))) >> BLOCK_SHIFT
        forced = (j == 0) | (j == tb) | (j == tb - 1)
        score = jnp.where(j > tb, NEG, jnp.where(forced, BIG, imp[0:rows]))
        score = jnp.where(j < n_blocks, score, -jnp.inf)
        for _ in range(min(SLC_TOPN, n_blocks)):
            top = jnp.max(score, axis=0, keepdims=True)
            first = jnp.min(jnp.where(score == top, j_f, float(LANES)), axis=0, keepdims=True)
            score = jnp.where(j_f == first, -jnp.inf, score)
        neg_sel = jnp.where((score == -jnp.inf) & (j < n_blocks), 0.0, NEG)
        if rows < LANES:
            neg_sel = jnp.concatenate([neg_sel, jnp.full((LANES - rows, HK * Q), NEG, F32)], axis=0)
        for hk in range(HK):
            sel_ref[0, hk] = neg_sel[:, hk * Q:(hk + 1) * Q].T.astype(sel_ref.dtype)

    few = q0 + Q <= (LANES // 2) * SLC_BLOCK

    @pl.when(few)
    def _():
        choose(LANES // 2)

    @pl.when(jnp.logical_not(few))
    def _():
        choose(LANES)


def _cmp_select(q, kc, vc, overlap, batch, seq):
    M = q.shape[0]
    Q = Q_BLOCK
    nq = seq // Q
    n_cmp = kc.shape[1]
    return pl.pallas_call(
        functools.partial(_cmp_select_kernel, n_blocks=seq // SLC_BLOCK),
        grid=(batch, nq),
        in_specs=[pl.BlockSpec((Q, NSA_W), lambda b, i: (b * nq + i, 0)),
                  pl.BlockSpec((1, n_cmp, KV_W), lambda b, i: (b, 0, 0)),
                  pl.BlockSpec((1, n_cmp, KV_W), lambda b, i: (b, 0, 0)),
                  pl.BlockSpec((LANES, n_cmp), lambda b, i: (0, 0))],
        out_specs=[pl.BlockSpec((Q, NSA_W), lambda b, i: (b * nq + i, 0)),
                   pl.BlockSpec((1, NSA_KV_HEADS, Q, LANES), lambda b, i: (b, 0, i, 0))],
        out_shape=[jax.ShapeDtypeStruct((M, NSA_W), F32),
                   jax.ShapeDtypeStruct((batch, NSA_KV_HEADS, seq, LANES), BF16)],
        scratch_shapes=[pltpu.VMEM((NSA_HEADS * Q, LANES), F32)],
        compiler_params=_params(2),
        name="nsa_cmp_select",
    )(q, kc, vc, overlap)


def _window_bias():
    r = np.arange(SLC_Q)[:, None]
    j = np.arange(WINDOW + SLC_Q)[None, :]
    band = (j > r) & (j <= r + WINDOW)
    tabs = [np.where(band & (j >= WINDOW - SLC_Q * v), 0.0, NEG) for v in range(WINDOW // SLC_Q + 1)]
    return np.stack(tabs).astype(np.float32)


def _slc_win_kernel(q_ref, ks_ref, vs_ref, kw_ref, vw_ref, oh_ref, wb_ref, sel_ref, gt_ref, oc_ref, o_ref,
                    m_s, acc_s):
    Q = SLC_Q
    G = NSA_GQA
    HK = NSA_KV_HEADS
    R = HK * G * Q
    RB = SLC_ROW_BLOCK
    KC = SLC_KEYS
    NW = WINDOW // Q + 1
    q0 = pl.program_id(1) * Q
    n_full = q0 // KC
    t_row = q0 + (lax.broadcasted_iota(jnp.int32, (RB, 1), 0) & (Q - 1))
    key_off = lax.broadcasted_iota(jnp.int32, (RB, KC), 1)
    half_q = lax.broadcasted_iota(jnp.int32, (Q, LANES), 1) >> HEAD_SHIFT
    qpad = _stacked_queries(q_ref)
    qaug =jnp.concatenate([qpad, jnp.concatenate([sel_ref[0, hk] for hk in range(HK) for _ in range(G)], axis=0)],
                           axis=1)
    m_s[...] = jnp.full(m_s.shape, NEG, F32)
    acc_s[...] = jnp.zeros(acc_s.shape, F32)

    def slc_chunk(c, causal):
        k0 = pl.multiple_of(c * KC, KC)
        kaug = jnp.concatenate([ks_ref[pl.ds(k0, KC), :], oh_ref[pl.ds(k0, KC), :]], axis=1)
        vaug = jnp.concatenate([vs_ref[pl.ds(k0, KC), :], jnp.ones((KC, LANES), BF16)], axis=1)
        for rb in range(R // RB):
            rows = slice(rb * RB, (rb + 1) * RB)
            s = _dot_nt(qaug[rows], kaug)
            if causal:
                s = jnp.where(k0 + key_off <= t_row, s, NEG)
            m_old = m_s[rows, :]
            m_new = jnp.maximum(m_old, jnp.max(s, axis=-1, keepdims=True))
            p = jnp.exp2(s - _tile_lanes(m_new, KC // LANES)).astype(BF16)
            acc_s[rows, :] = _tile_lanes(jnp.exp2(m_old - m_new), 2) * acc_s[rows, :] + _dot(p, vaug)
            m_s[rows, :] = m_new

    def run(c0, n):
        for i in range(n):
            slc_chunk(c0 + i, False)

    def body(i, carry):
        run(i * SLC_GROUP, SLC_GROUP)
        return carry

    lax.fori_loop(0, n_full // SLC_GROUP, body, 0)
    base = n_full - n_full % SLC_GROUP
    size = SLC_GROUP // 2
    while size >= 1:
        rest = n_full % (2 * size)

        @pl.when(rest >= size)
        def _(base=base, size=size):
            run(base, size)

        base = base + jnp.where(rest >= size, size, 0)
        size //= 2
    slc_chunk(n_full, True)
    acc = acc_s[...]
    o_slc = acc[:, 0:LANES] / acc[:, LANES:2 * LANES]

    rows_k, rows_v = [], []
    for w in range(NW):
        k0 = pl.multiple_of(jnp.maximum(q0 - WINDOW + w * Q, 0), Q)
        rows_k.append(kw_ref[pl.ds(k0, Q), :])
        rows_v.append(vw_ref[pl.ds(k0, Q), :])
    k_win = jnp.concatenate(rows_k, axis=0)
    v_win = jnp.concatenate([jnp.concatenate(rows_v, axis=0), jnp.ones((NW * Q, LANES), BF16)], axis=1)
    s = _dot_nt(qpad, k_win) + jnp.concatenate([wb_ref[0]] * (HK * G), axis=0)
    p = jnp.exp2(s - _tile_lanes(_row_max(s), NW * Q // LANES)).astype(BF16)
    acc = _dot(p, v_win)
    o_win = acc[:, 0:LANES] / acc[:, LANES:2 * LANES]

    gates = gt_ref[...]
    for g in range(G):
        oc = oc_ref[:, g * LANES:(g + 1) * LANES]
        tiles = []
        for hk in range(HK):
            c0 = 3 * (hk * G + g)
            rows = slice((hk * G + g) * Q, (hk * G + g + 1) * Q)
            tiles.append(gates[:, c0:c0 + 1] * oc + gates[:, c0 + 1:c0 + 2] * o_slc[rows]
                         + gates[:, c0 + 2:c0 + 3] * o_win[rows])
        o_ref[:, g * LANES:(g + 1) * LANES] = jnp.where(half_q == 0, tiles[0], tiles[1]).astype(o_ref.dtype)


def _slc_win(q, k, v, onehot, wbias, sel, gates, o_cmp, batch, seq):
    M = q.shape[0]
    Q = SLC_Q
    G = NSA_GQA
    nq = seq // Q
    n_var = wbias.shape[0]
    qmap = lambda b, i: (b * nq + i, 0)
    return pl.pallas_call(
        _slc_win_kernel,
        grid=(batch, nq),
        in_specs=[pl.BlockSpec((Q, NSA_W), qmap),
                  pl.BlockSpec((seq, KV_W), lambda b, i: (b, 0)),
                  pl.BlockSpec((seq, KV_W), lambda b, i: (b, 0)),
                  pl.BlockSpec((seq, KV_W), lambda b, i: (b, 1)),
                  pl.BlockSpec((seq, KV_W), lambda b, i: (b, 1)),
                  pl.BlockSpec((seq, LANES), lambda b, i: (0, 0)),
                  pl.BlockSpec((1,) + wbias.shape[1:], lambda b, i: (jnp.minimum(i, n_var - 1), 0, 0)),
                  pl.BlockSpec((1, NSA_KV_HEADS, Q, LANES), lambda b, i: (b, 0, i, 0)),
                  pl.BlockSpec((Q, LANES), qmap),
                  pl.BlockSpec((Q, NSA_W), qmap)],
        out_specs=pl.BlockSpec((Q, NSA_W), qmap),
        out_shape=jax.ShapeDtypeStruct((M, NSA_W), BF16),
        scratch_shapes=[pltpu.VMEM((NSA_KV_HEADS * G * Q, LANES), F32),
                        pltpu.VMEM((NSA_KV_HEADS * G * Q, 2 * LANES), F32)],
        compiler_params=_params(2),
        name="nsa_slc_win",
    )(q, k, v, k, v, onehot, wbias, sel, gates, o_cmp)


def _outproj_kernel(oh_ref, on_ref, og_ref, w_ref, x_ref, g_ref, lw_ref, lb_ref, o_ref, *, alpha):
    mix = (_dot(oh_ref[...], w_ref[0:HG_W, :])
           + _dot(on_ref[...], w_ref[HG_W:HG_W + NSA_W, :])
           + _dot(og_ref[...], w_ref[HG_W + NSA_W:HG_W + NSA_W + GM_W, :]))
    y = alpha * x_ref[...] + (1.0 + g_ref[0]) * mix
    o_ref[...] = _layer_norm(y, lw_ref[...], lb_ref[...])


def _out_projection(o_h, o_n, o_g, w_o, x2, g1, ln_w, ln_b, seq, alpha):
    M, D = x2.shape
    tm = ROW_TILE
    per_b = seq // tm
    row = lambda i: (i, 0)
    const = lambda i: (0, 0)
    return pl.pallas_call(
        functools.partial(_outproj_kernel, alpha=alpha),
        grid=(M // tm,),
        in_specs=[pl.BlockSpec((tm, HG_W), row),
                  pl.BlockSpec((tm, NSA_W), row),
                  pl.BlockSpec((tm, GM_W), row),
                  pl.BlockSpec(w_o.shape, const),
                  pl.BlockSpec((tm, D), row),
                  pl.BlockSpec((1, 1, D), lambda i: (i // per_b, 0, 0)),
                  pl.BlockSpec((1, D), const),
                  pl.BlockSpec((1, D), const)],
        out_specs=pl.BlockSpec((tm, D), row),
        out_shape=jax.ShapeDtypeStruct((M, D), F32),
        compiler_params=_params(1),
        name="out_proj_ln",
    )(o_h, o_n, o_g, w_o, x2, g1, ln_w, ln_b)


def _ffn_kernel(x_ref, sc_ref, sh_ref, g_ref, w1_ref, w2_ref, lw_ref, lb_ref, o_ref, *, alpha, ff_chunk):
    x = x_ref[...]
    h = (x * (1.0 + sc_ref[0]) + sh_ref[0]).astype(BF16)
    y = jnp.zeros(x.shape, F32)
    for c in range(w1_ref.shape[1] // ff_chunk):
        a = jnp.maximum(_dot(h, w1_ref[:, c * ff_chunk:(c + 1) * ff_chunk]), 0.0)
        y = y + _dot((a * a).astype(BF16), w2_ref[c * ff_chunk:(c + 1) * ff_chunk, :])
    z = alpha * x + (1.0 + g_ref[0]) * y
    o_ref[...] = _layer_norm(z, lw_ref[...], lb_ref[...])


def _ffn(x2, sc, sh, g2, w1, w2, ln_w, ln_b, seq, alpha):
    M, D = x2.shape
    tm = ROW_TILE
    per_b = seq // tm
    row = lambda i: (i, 0)
    const = lambda i: (0, 0)
    bmap = lambda i: (i // per_b, 0, 0)
    return pl.pallas_call(
        functools.partial(_ffn_kernel, alpha=alpha, ff_chunk=1024),
        grid=(M // tm,),
        in_specs=[pl.BlockSpec((tm, D), row),
                  pl.BlockSpec((1, 1, D), bmap),
                  pl.BlockSpec((1, 1, D), bmap),
                  pl.BlockSpec((1, 1, D), bmap),
                  pl.BlockSpec(w1.shape, const),
                  pl.BlockSpec(w2.shape, const),
                  pl.BlockSpec((1, D), const),
                  pl.BlockSpec((1, D), const)],
        out_specs=pl.BlockSpec((tm, D), row),
        out_shape=jax.ShapeDtypeStruct((M, D), F32),
        compiler_params=_params(1),
        name="ffn_ln",
    )(x2, sc, sh, g2, w1, w2, ln_w, ln_b)


def kernel(x, c, positions, w_in, w_o, hgrn_lower_bounds, hgrn_norm_w, cmp_pe_k, cmp_w1_k, cmp_w2_k, cmp_pe_v, cmp_w1_v, cmp_w2_v, gmlp_norm_w, gmlp_norm_b, gmlp_w_s, gmlp_b_s, w_ff1, w_ff2, w_ada, b_ada, ln1_w, ln1_b, ln2_w, ln2_b):
    B, S, D = x.shape
    L = w_in.shape[0]
    M = B * S
    alpha = (2 * L) ** 0.25
    n_seg = S // CMP_STRIDE

    inv_freq = ROPE_THETA ** (-jnp.arange(0, HEAD_DIM, 2, dtype=F32) / HEAD_DIM)
    ang = positions.astype(F32)[..., None] * inv_freq
    cos, sin = jnp.cos(ang).reshape(M, HEAD_DIM // 2), jnp.sin(ang).reshape(M, HEAD_DIM // 2)
    cos, sin = lax.optimization_barrier((cos, sin))
    cos_f = jnp.concatenate([cos, cos, cos, cos], axis=-1)
    sin_s = jnp.concatenate([-sin, sin, -sin, sin], axis=-1)

    lb_sm = jax.nn.softmax(hgrn_lower_bounds.astype(F32), axis=0)
    lb_all = jnp.cumsum(lb_sm, axis=0) - lb_sm[0:1]

    col_idx, n_cols = _in_col_index()
    assert n_cols == w_in.shape[2]
    step = np.where(col_idx >= 0, 1, 0)
    cuts = [0] + [i for i in range(1, len(col_idx)) if col_idx[i] != col_idx[i - 1] + step[i]] + [len(col_idx)]
    pieces = [w_in[:, :, col_idx[a]:col_idx[b - 1] + 1].astype(BF16) if col_idx[a] >= 0
              else jnp.zeros(w_in.shape[:2] + (b - a,), BF16) for a, b in zip(cuts[:-1], cuts[1:])]
    w_cat = jnp.concatenate(pieces, axis=2)

    lane_head = np.arange(HG_W) // HEAD_DIM
    bd = jnp.asarray((lane_head[:, None] == lane_head[None, :]).astype(np.float32), dtype=BF16)
    hg_window, hg_pair_mask = _hgrn_tables()
    hg_window = jnp.asarray(hg_window, dtype=BF16)
    hg_pair_mask = jnp.asarray(hg_pair_mask, dtype=F32)
    cs = np.arange(n_seg) * CMP_STRIDE
    ss = np.arange(LANES) * SLC_BLOCK
    ov = (cs[:, None] < ss[None, :] + SLC_BLOCK) & (ss[None, :] < cs[:, None] + CMP_LEN)
    ov &= (np.arange(n_seg) < n_seg - 1)[:, None] & (np.arange(LANES) < S // SLC_BLOCK)[None, :]
    overlap = jnp.asarray(ov.T.astype(np.float32), dtype=BF16)
    assert S % SLC_KEYS == 0 and S // SLC_BLOCK <= LANES
    onehot = (jnp.arange(S)[:, None] // SLC_BLOCK == jnp.arange(LANES)[None, :]).astype(BF16)
    wbias = jnp.asarray(_window_bias())
    head_rows = [w_o[:, HG_W + h * HEAD_DIM:HG_W + (h + 1) * HEAD_DIM]
                 for h in _nsa_head_perm()[::HEAD_DIM] // HEAD_DIM]
    w_o = jnp.concatenate([w_o[:, :HG_W]] + head_rows + [w_o[:, HG_W + NSA_W:]], axis=1).astype(BF16)

    c8 = jnp.pad(c, ((0, 8 - B), (0, 0)))
    mod = _modulation(c8, w_ada, b_ada)[:, :B]

    x2 = x.reshape(M, D)
    for l in range(L):
        sh1, sc1, g1, sh2, sc2, g2 = [mod[l, :, i * D:(i + 1) * D].reshape(B, 1, D) for i in range(6)]
        hg, q, k, v, k_cmp, v_cmp, gm, gates = _in_projection(x2, sc1, sh1, w_cat[l], cos_f, sin_s, S)

        o_h = _hgrn2(hg, lb_all[l].reshape(1, HG_W), jnp.tile(hgrn_norm_w[l], HG_HEADS).reshape(1, HG_W), bd,
                     hg_window, hg_pair_mask, S)

        bias = jnp.repeat(gmlp_b_s[l].T, HEAD_DIM, axis=1)
        o_g = _gmlp(gm, gmlp_norm_w[l].reshape(1, GM_W), gmlp_norm_b[l].reshape(1, GM_W), gmlp_w_s[l], bias)

        pe_k = jnp.pad(cmp_pe_k[l].reshape(1, -1), ((0, 7), (0, 0)))
        pe_v = jnp.pad(cmp_pe_v[l].reshape(1, -1), ((0, 7), (0, 0)))
        seg_shape = (NSA_KV_HEADS, B, n_seg, CMP_STRIDE * HEAD_DIM)
        kc = _compress(k_cmp.reshape(seg_shape), pe_k, cmp_w1_k[l].astype(BF16), cmp_w2_k[l].astype(BF16))
        vc = _compress(v_cmp.reshape(seg_shape), pe_v, cmp_w1_v[l].astype(BF16), cmp_w2_v[l].astype(BF16))
        o_cmp, sel = _cmp_select(q, kc, vc, overlap, B, S)
        o_n = _slc_win(q, k, v, onehot, wbias, sel, gates, o_cmp, B, S)

        x2 = _out_projection(o_h, o_n, o_g, w_o[l].astype(BF16), x2, g1,
                             ln1_w[l].reshape(1, D), ln1_b[l].reshape(1, D), S, alpha)
        x2 = _ffn(x2, sc2, sh2, g2, w_ff1[l].astype(BF16), w_ff2[l].astype(BF16),
                  ln2_w[l].reshape(1, D), ln2_b[l].reshape(1, D), S, alpha)
    return x2.reshape(B, S, D)
```
